```python
import jax, jax.numpy as jnp
from jax import lax
import numpy as np

D_MODEL = 4096
BATCH = 4
SEQ = 2048
DEPTH = 4
DEC_BATCH = 8
DEC_SEQ = 1
PAST_LEN = 8192
PAGE_SIZE = 128

ATT_GROUPS = ((128, 1), (512, 4), (2048, 16))
N_GROUPS = 3
HEADS_PER_GROUP = 8
HEAD_DIM = 128
ATT_WIDTH = HEADS_PER_GROUP * HEAD_DIM
ROPE_THETA = 10000.0
NEG_INF = -1e30
RWKV_HEAD = 64
RWKV_HEADS = 32
RWKV_WIDTH = RWKV_HEADS * RWKV_HEAD
LORA_W = 128
LORA_A = 128
LORA_G = 480
GN_EPS = 64e-5
D_FF = 4 * D_MODEL
NORM_EPS = 1e-6
ATT_COLS = 3 * N_GROUPS * ATT_WIDTH
RWKV_COLS = 3 * RWKV_WIDTH + LORA_W + LORA_A + LORA_G
GATE_COLS = 2 * D_MODEL
N_IN = ATT_COLS + RWKV_COLS + GATE_COLS
RWKV_SPLITS = (RWKV_WIDTH, 2 * RWKV_WIDTH, 3 * RWKV_WIDTH, 3 * RWKV_WIDTH + LORA_W, 3 * RWKV_WIDTH + LORA_W + LORA_A)

kernel_name = 'dilated_swa_rwkv7_gated_hybrid_step'


def _rms(x, g, eps=NORM_EPS):
    xf = x.astype(jnp.float32)
    y = xf * lax.rsqrt(jnp.mean(xf * xf, axis=-1, keepdims=True) + eps)
    return (y * g.astype(jnp.float32)).astype(x.dtype)


def _rope(x, pos):
    half = HEAD_DIM // 2
    inv_freq = ROPE_THETA ** (-jnp.arange(half, dtype=jnp.float32) / half)
    ang = pos.astype(jnp.float32)[:, None] * inv_freq[None, :]
    shape = (ang.shape[0],) + (1,) * (x.ndim - 3) + (half,)
    cos = jnp.cos(ang).reshape(shape)
    sin = jnp.sin(ang).reshape(shape)
    xf = x.astype(jnp.float32)
    x1, x2 = xf[..., :half], xf[..., half:]
    return jnp.concatenate([x1 * cos - x2 * sin, x2 * cos + x1 * sin], axis=-1).astype(x.dtype)


def _dilated_band_attention(q, k, v, window, dilation):
    B, S, H, hd = q.shape
    span = window // dilation
    L = S // dilation
    nblk = -(-L // span)
    Lp = nblk * span

    def to_res(t):
        return t.reshape(B, L, dilation, H, hd).transpose(0, 2, 1, 3, 4)

    qb = jnp.pad(to_res(q), ((0, 0), (0, 0), (0, Lp - L), (0, 0), (0, 0))).reshape(B, dilation, nblk, span, H, hd)

    def band(t):
        tp = jnp.pad(to_res(t), ((0, 0), (0, 0), (span, Lp - L), (0, 0), (0, 0)))
        tp = tp.reshape(B, dilation, nblk + 1, span, H, hd)
        return jnp.concatenate([tp[:, :, :-1], tp[:, :, 1:]], axis=3)

    kb, vb = band(k), band(v)
    s = jnp.einsum('brnqhd,brnkhd->brnhqk', qb, kb, preferred_element_type=jnp.float32) * (hd ** -0.5)
    qm = jnp.arange(nblk)[:, None] * span + jnp.arange(span)[None, :]
    km = (jnp.arange(nblk)[:, None] - 1) * span + jnp.arange(2 * span)[None, :]
    diff = qm[:, :, None] - km[:, None, :]
    valid = (km[:, None, :] >= 0) & (diff >= 0) & (diff <= span)
    s = jnp.where(valid[None, None, :, None], s, NEG_INF)
    lse = jax.nn.logsumexp(s, axis=-1)
    p = jnp.exp(s - lse[..., None])
    o = jnp.einsum('brnhqk,brnkhd->brnqhd', p, vb.astype(jnp.float32))
    o = o.reshape(B, dilation, Lp, H, hd)[:, :, :L].transpose(0, 2, 1, 3, 4).reshape(B, S, H, hd)
    lse = lse.transpose(0, 1, 2, 4, 3).reshape(B, dilation, Lp, H)[:, :, :L].transpose(0, 2, 1, 3).reshape(B, S, H)
    return o, lse


def _dilated_cached_attention(q, k, v, buf, window, dilation):
    B, T, H, hd = q.shape
    Lb = buf.shape[1]
    kv = jnp.concatenate([buf.astype(q.dtype), jnp.stack([k, v], axis=2)], axis=1)
    span = window // dilation
    idx = Lb + jnp.arange(T)[:, None] - dilation * jnp.arange(span + 1)[None, :]
    valid = idx >= 0
    sel = jnp.take(kv, jnp.maximum(idx, 0), axis=1)
    s = jnp.einsum('bthd,btjhd->bhtj', q, sel[:, :, :, 0], preferred_element_type=jnp.float32) * (hd ** -0.5)
    s = jnp.where(valid[None, None], s, NEG_INF)
    lse = jax.nn.logsumexp(s, axis=-1)
    p = jnp.exp(s - lse[..., None])
    o = jnp.einsum('bhtj,btjhd->bthd', p, sel[:, :, :, 1].astype(jnp.float32))
    return o, jnp.transpose(lse, (0, 2, 1)), kv[:, T:]


def _wkv_step(state, inp):
    r, w, k, v, kk, a = inp
    sa = jnp.einsum('bhij,bhj->bhi', state, -kk)
    state = (state * w[:, :, None, :]
             + jnp.einsum('bhi,bhj->bhij', v, k)
             + jnp.einsum('bhi,bhj->bhij', sa, kk * a))
    y = jnp.einsum('bhij,bhj->bhi', state, r)
    return state, y


def _rwkv7_branch(z, shift0, wkv0, w):
    B, T, _ = z.shape
    f32 = jnp.float32
    z_prev = jnp.concatenate([shift0[:, None, :].astype(z.dtype), z[:, :-1]], axis=1)
    zs = z + w['shift_mu'] * (z_prev - z)
    r, k, v, w_lo, a_lo, g_lo = jnp.split(zs, RWKV_SPLITS, axis=-1)
    log_w = -jax.nn.softplus(-(w['decay_w0'] + jnp.tanh(w_lo) @ w['decay_up']).astype(f32)) - 0.5
    decay = jnp.exp(-jnp.exp(log_w))
    a = jax.nn.sigmoid((w['iclr_a0'] + a_lo @ w['iclr_up']).astype(f32))
    g = jax.nn.sigmoid(g_lo) @ w['gate_up']
    k = k.astype(f32)
    kk = (k * w['key_k']).reshape(B, T, RWKV_HEADS, RWKV_HEAD)
    kk = kk * lax.rsqrt(jnp.maximum(jnp.sum(kk * kk, axis=-1, keepdims=True), 1e-24))
    k = k * (1.0 + (a - 1.0) * w['key_a'])
    heads = lambda t: t.astype(f32).reshape(B, T, RWKV_HEADS, RWKV_HEAD)
    r_h, k_h, v_h, a_h, w_h = heads(r), heads(k), heads(v), heads(a), heads(decay)
    xs = tuple(jnp.moveaxis(t, 1, 0) for t in (r_h, w_h, k_h, v_h, kk, a_h))
    wkv_new, y = lax.scan(_wkv_step, wkv0.astype(f32), xs)
    y = jnp.moveaxis(y, 0, 1)
    mu = jnp.mean(y, axis=-1, keepdims=True)
    var = jnp.mean(jnp.square(y - mu), axis=-1, keepdims=True)
    y = ((y - mu) * lax.rsqrt(var + GN_EPS)).reshape(B, T, RWKV_WIDTH) * w['lnx_g'] + w['lnx_b']
    bonus = jnp.sum(r_h * k_h * w['bonus_rk'], axis=-1, keepdims=True) * v_h
    y = (y + bonus.reshape(B, T, RWKV_WIDTH)) * g
    return y.astype(z.dtype), wkv_new.astype(wkv0.dtype), z[:, -1]


def _mixer_block(x, pos, kv_bufs, wkv0, shift0, w):
    B, T, _ = x.shape
    xn = _rms(x, w['norm1_g'])
    proj = xn @ w['w_in']
    att_z, rwkv_z, gate_z = jnp.split(proj, (ATT_COLS, ATT_COLS + RWKV_COLS), axis=-1)
    qkv = att_z.reshape(B, T, 3, N_GROUPS, HEADS_PER_GROUP, HEAD_DIM)
    q = _rope(_rms(qkv[:, :, 0], w['q_norm_g']), pos)
    k = _rope(_rms(qkv[:, :, 1], w['k_norm_g']), pos)
    v = qkv[:, :, 2]
    outs, lses, new_bufs = [], [], []
    for gi, (win, dil) in enumerate(ATT_GROUPS):
        if kv_bufs is None:
            o, lse = _dilated_band_attention(q[:, :, gi], k[:, :, gi], v[:, :, gi], win, dil)
            keep = min(win, T)
            new_bufs.append(jnp.stack([k[:, T - keep:, gi], v[:, T - keep:, gi]], axis=2))
        else:
            o, lse, nb = _dilated_cached_attention(q[:, :, gi], k[:, :, gi], v[:, :, gi], kv_bufs[gi], win, dil)
            new_bufs.append(nb)
        outs.append(o)
        lses.append(lse)
    alpha = jax.nn.softmax(jnp.stack(lses, axis=2), axis=2)
    att = jnp.einsum('btgh,btghd->bthd', alpha, jnp.stack(outs, axis=2))
    y_att = att.astype(x.dtype).reshape(B, T, ATT_WIDTH) @ w['w_att_o']
    y_rwkv, wkv_new, shift_new = _rwkv7_branch(rwkv_z, shift0, wkv0, w)
    y_rwkv = y_rwkv @ w['w_rwkv_o']
    g_att, g_rwkv = jnp.split(gate_z, 2, axis=-1)
    merged = jax.nn.sigmoid(g_att) * y_att + jax.nn.sigmoid(g_rwkv) * y_rwkv
    x = x + (merged @ w['w_out']).astype(x.dtype)
    h = _rms(x, w['norm2_g']) @ w['w_ffn_up']
    x = x + (jnp.square(jax.nn.relu(h)) @ w['w_ffn_down']).astype(x.dtype)
    return x, new_bufs, wkv_new, shift_new


def setup_inputs(seed: int = 0) -> dict:
    key = jax.random.key(seed)
    ks = jax.random.split(key, 32)
    f32 = jnp.float32

    def nrm(i, shape, scale):
        return jax.random.normal(ks[i], shape, f32) * scale

    def gain(i, shape):
        return 1.0 + nrm(i, shape, 0.02)

    L = DEPTH

    def kv_shape(win):
        return (L, DEC_BATCH, min(win, PAST_LEN), 2, HEADS_PER_GROUP, HEAD_DIM)

    return {
        'x_prompt': nrm(0, (BATCH, SEQ, D_MODEL), 1.0),
        'x_sample': nrm(1, (DEC_BATCH, DEC_SEQ, D_MODEL), 1.0),
        'cache_kv_w128': nrm(2, kv_shape(ATT_GROUPS[0][0]), 1.0),
        'cache_kv_w512': nrm(3, kv_shape(ATT_GROUPS[1][0]), 1.0),
        'cache_kv_w2048': nrm(4, kv_shape(ATT_GROUPS[2][0]), 1.0),
        'state_wkv': nrm(5, (L, DEC_BATCH, RWKV_HEADS, RWKV_HEAD, RWKV_HEAD), 0.3),
        'state_shift': nrm(6, (L, DEC_BATCH, RWKV_COLS), 1.0),
        'norm1_g': gain(7, (L, D_MODEL)),
        'w_in': nrm(8, (L, D_MODEL, N_IN), D_MODEL ** -0.5),
        'q_norm_g': gain(9, (L, HEAD_DIM)),
        'k_norm_g': gain(10, (L, HEAD_DIM)),
        'shift_mu': jax.random.uniform(ks[11], (L, RWKV_COLS), f32, 0.0, 1.0),
        'decay_w0': jax.random.uniform(ks[12], (L, RWKV_WIDTH), f32, -6.0, 1.0),
        'decay_up': nrm(13, (L, LORA_W, RWKV_WIDTH), 0.5 * LORA_W ** -0.5),
        'iclr_a0': nrm(14, (L, RWKV_WIDTH), 0.5),
        'iclr_up': nrm(15, (L, LORA_A, RWKV_WIDTH), 0.5 * LORA_A ** -0.5),
        'gate_up': nrm(16, (L, LORA_G, RWKV_WIDTH), LORA_G ** -0.5),
        'key_k': 1.0 + nrm(17, (L, RWKV_WIDTH), 0.1),
        'key_a': 1.0 + nrm(18, (L, RWKV_WIDTH), 0.1),
        'bonus_rk': nrm(19, (L, RWKV_HEADS, RWKV_HEAD), 0.05),
        'lnx_g': gain(20, (L, RWKV_WIDTH)),
        'lnx_b': nrm(21, (L, RWKV_WIDTH), 0.02),
        'w_att_o': nrm(22, (L, ATT_WIDTH, D_MODEL), ATT_WIDTH ** -0.5),
        'w_rwkv_o': nrm(23, (L, RWKV_WIDTH, D_MODEL), RWKV_WIDTH ** -0.5),
        'w_out': nrm(24, (L, D_MODEL, D_MODEL), D_MODEL ** -0.5),
        'norm2_g': gain(25, (L, D_MODEL)),
        'w_ffn_up': nrm(26, (L, D_MODEL, D_FF), D_MODEL ** -0.5),
        'w_ffn_down': nrm(27, (L, D_FF, D_MODEL), D_FF ** -0.5),
    }


def reference(x_prompt, x_sample, cache_kv_w128, cache_kv_w512, cache_kv_w2048, state_wkv, state_shift,
              norm1_g, w_in, q_norm_g, k_norm_g, shift_mu, decay_w0, decay_up, iclr_a0, iclr_up, gate_up,
              key_k, key_a, bonus_rk, lnx_g, lnx_b, w_att_o, w_rwkv_o, w_out, norm2_g, w_ffn_up, w_ffn_down):
    B, S, _ = x_prompt.shape
    T = x_sample.shape[1]
    pos_p = jnp.arange(S, dtype=jnp.int32)
    pos_s = PAST_LEN + jnp.arange(T, dtype=jnp.int32)
    caches = (cache_kv_w128, cache_kv_w512, cache_kv_w2048)
    zeros_wkv = jnp.zeros((B, RWKV_HEADS, RWKV_HEAD, RWKV_HEAD), x_prompt.dtype)
    zeros_shift = jnp.zeros((B, RWKV_COLS), x_prompt.dtype)
    yp, ys = x_prompt, x_sample
    p_bufs, p_wkv, p_shift, s_bufs, s_wkv, s_shift = [], [], [], [], [], []
    for l in range(DEPTH):
        w = {
            'norm1_g': norm1_g[l], 'w_in': w_in[l], 'q_norm_g': q_norm_g[l], 'k_norm_g': k_norm_g[l],
            'shift_mu': shift_mu[l], 'decay_w0': decay_w0[l], 'decay_up': decay_up[l],
            'iclr_a0': iclr_a0[l], 'iclr_up': iclr_up[l], 'gate_up': gate_up[l],
            'key_k': key_k[l], 'key_a': key_a[l], 'bonus_rk': bonus_rk[l],
            'lnx_g': lnx_g[l], 'lnx_b': lnx_b[l], 'w_att_o': w_att_o[l], 'w_rwkv_o': w_rwkv_o[l],
            'w_out': w_out[l], 'norm2_g': norm2_g[l], 'w_ffn_up': w_ffn_up[l], 'w_ffn_down': w_ffn_down[l],
        }
        yp, bufs_p, wkv_p, sh_p = _mixer_block(yp, pos_p, None, zeros_wkv, zeros_shift, w)
        ys, bufs_s, wkv_s, sh_s = _mixer_block(ys, pos_s, tuple(c[l] for c in caches), state_wkv[l], state_shift[l], w)
        p_bufs.append(bufs_p)
        p_wkv.append(wkv_p)
        p_shift.append(sh_p)
        s_bufs.append(bufs_s)
        s_wkv.append(wkv_s)
        s_shift.append(sh_s)
    new_kv_w128_prompt = jnp.stack([b[0] for b in p_bufs])
    new_kv_w512_prompt = jnp.stack([b[1] for b in p_bufs])
    new_kv_w2048_prompt = jnp.stack([b[2] for b in p_bufs])
    new_wkv_prompt = jnp.stack(p_wkv)
    new_shift_prompt = jnp.stack(p_shift)
    new_kv_w128_sample = jnp.stack([b[0] for b in s_bufs])
    new_kv_w512_sample = jnp.stack([b[1] for b in s_bufs])
    new_kv_w2048_sample = jnp.stack([b[2] for b in s_bufs])
    new_wkv_sample = jnp.stack(s_wkv)
    new_shift_sample = jnp.stack(s_shift)
    return (yp, ys, new_kv_w128_prompt, new_kv_w512_prompt, new_kv_w2048_prompt, new_wkv_prompt, new_shift_prompt,
            new_kv_w128_sample, new_kv_w512_sample, new_kv_w2048_sample, new_wkv_sample, new_shift_sample)
```

```python
import functools
import math

import jax
import jax.numpy as jnp
from jax import lax
from jax.experimental import pallas as pl
from jax.experimental.pallas import tpu as pltpu

F32 = jnp.float32
BF16 = jnp.bfloat16

N_GROUPS = 3
ATT_GROUPS = ((128, 1), (512, 4), (2048, 16))
HEAD_DIM = 128
RWKV_HEAD = 64
LORA_W = 128
LORA_A = 128
LORA_G = 480
ROPE_THETA = 10000.0
PAST_LEN = 8192
NEG_INF = -1e30
GN_EPS = 64e-5
NORM_EPS = 1e-6

LANES = 128
VMEM_LIMIT_BYTES = 56 * 1024 * 1024

CHUNK = 64
LORA_G_PAD = 512
SPAN = 128


def _cparams(sem):
    return pltpu.CompilerParams(dimension_semantics=sem, vmem_limit_bytes=VMEM_LIMIT_BYTES)


def _pick(n, prefs):
    for p in prefs:
        if n % p == 0:
            return p
    return n


def _split(x, n):
    parts, rem = [], x
    for i in range(n):
        p = rem.astype(BF16)
        parts.append(p)
        if i + 1 < n:
            rem = rem - p.astype(F32)
    return parts


_NN = (((1,), (0,)), ((), ()))
_NT = (((1,), (1,)), ((), ()))
_TN = (((0,), (0,)), ((), ()))


def _mmp(a_parts, b_parts, dims=_NN):
    order = max(len(a_parts), len(b_parts))
    acc = None
    for i, a in enumerate(a_parts):
        for j, b in enumerate(b_parts):
            if i + j < order:
                t = lax.dot_general(a, b, dims, preferred_element_type=F32)
                acc = t if acc is None else acc + t
    return acc


def _rms_kernel(x_ref, g_ref, o_ref):
    x = x_ref[...]
    ms = jnp.mean(x * x, axis=-1, keepdims=True)
    o_ref[...] = (x * lax.rsqrt(ms + NORM_EPS) * g_ref[...]).astype(o_ref.dtype)


def _rmsnorm(x, g, l):
    M, D = x.shape
    tm = _pick(M, (256, 128, 64, 32, 16, 8))
    return pl.pallas_call(
        _rms_kernel,
        grid=(M // tm,),
        in_specs=[pl.BlockSpec((tm, D), lambda i: (i, 0)),
                  pl.BlockSpec((None, 1, D), lambda i: (l, 0, 0))],
        out_specs=pl.BlockSpec((tm, D), lambda i: (i, 0)),
        out_shape=jax.ShapeDtypeStruct((M, D), BF16),
        compiler_params=_cparams(("parallel",)),
        name="rmsnorm",
    )(x, g.reshape(g.shape[0], 1, D))


def _ep_none(acc):
    return acc


def _ep_relu2(acc):
    return jnp.square(jnp.maximum(acc, 0.0))


def _ep_gate(acc, g):
    return jax.nn.sigmoid(g) * acc


def _ep_gate_add(acc, g, t):
    return t + jax.nn.sigmoid(g) * acc


def _ep_residual(acc, x):
    return x + acc


def _mm_kernel(*refs, nk, n_extra, epilogue):
    a_ref, w_ref = refs[0], refs[1]
    extra = refs[2:2 + n_extra]
    o_ref = refs[2 + n_extra]
    if nk == 1:
        acc = jnp.dot(a_ref[...], w_ref[...], preferred_element_type=F32)
        o_ref[...] = epilogue(acc, *[e[...] for e in extra]).astype(o_ref.dtype)
        return
    acc_ref = refs[3 + n_extra]
    k = pl.program_id(2)

    @pl.when(k == 0)
    def _():
        acc_ref[...] = jnp.zeros_like(acc_ref)

    acc_ref[...] += jnp.dot(a_ref[...], w_ref[...], preferred_element_type=F32)

    @pl.when(k == nk - 1)
    def _():
        o_ref[...] = epilogue(acc_ref[...], *[e[...] for e in extra]).astype(o_ref.dtype)


def _matmul(a, w, *, out_dtype, epilogue=_ep_none, extras=(), tm=None, tn=None, tk=None, name="matmul"):
    M, K = a.shape
    N = w.shape[1]
    tm = tm or _pick(M, (1024, 512, 256, 128, 64, 32, 16))
    tn = tn or _pick(N, (1024, 512, 256, 128))
    tk = tk or _pick(K, (4096, 2048, 1024, 512, 256, 128))
    nk = K // tk
    in_specs = [pl.BlockSpec((tm, tk), lambda i, j, k: (i, k)),
                pl.BlockSpec((tk, tn), lambda i, j, k: (k, j))]
    args = [a, w]
    for arr, off in extras:
        assert off % tn == 0
        in_specs.append(pl.BlockSpec((tm, tn), lambda i, j, k, _o=off // tn: (i, j + _o)))
        args.append(arr)
    scratch = [] if nk == 1 else [pltpu.VMEM((tm, tn), F32)]
    return pl.pallas_call(
        functools.partial(_mm_kernel, nk=nk, n_extra=len(extras), epilogue=epilogue),
        grid=(M // tm, N // tn, nk),
        in_specs=in_specs,
        out_specs=pl.BlockSpec((tm, tn), lambda i, j, k: (i, j)),
        out_shape=jax.ShapeDtypeStruct((M, N), out_dtype),
        scratch_shapes=scratch,
        compiler_params=_cparams(("parallel", "parallel", "arbitrary")),
        name=name,
    )(*args)


def _norm_rope(x, g, cos, sin):
    ms = jnp.mean(x * x, axis=-1, keepdims=True)
    y = x * lax.rsqrt(ms + NORM_EPS) * g
    return y * cos + pltpu.roll(y, HEAD_DIM // 2, 1) * sin


def _band_attn_kernel(q_ref, k_ref, v_ref, cos_ref, sin_ref, qg_ref, kg_ref,
                      o_ref, lse_ref, knew_ref, *, L):
    cos = cos_ref[...]
    sin = sin_ref[...]
    q = _norm_rope(q_ref[0], qg_ref[...], cos, sin)
    k = _norm_rope(k_ref[0], kg_ref[...], cos, sin)
    knew_ref[0] = k[L - SPAN:]
    qb = q.astype(BF16)
    kb = k.astype(BF16)
    vb = v_ref[0].astype(BF16)
    scale = HEAD_DIM ** -0.5
    for i in range(L // SPAN):
        lo = max(i - 1, 0) * SPAN
        hi = (i + 1) * SPAN
        nkeys = hi - lo
        s = lax.dot_general(qb[i * SPAN:hi], kb[lo:hi], _NT, preferred_element_type=F32) * scale
        qi = lax.broadcasted_iota(jnp.int32, (SPAN, nkeys), 0) + (i * SPAN - lo)
        ki = lax.broadcasted_iota(jnp.int32, (SPAN, nkeys), 1)
        valid = (ki <= qi) & (ki >= qi - SPAN)
        s = jnp.where(valid, s, NEG_INF)
        m = jnp.max(s, axis=-1, keepdims=True)
        p = jnp.exp(s - m)
        den = jnp.sum(p, axis=-1, keepdims=True)
        o = jnp.dot(p.astype(BF16), vb[lo:hi], preferred_element_type=F32)
        o_ref[0, i * SPAN:hi, :] = o / den
        lse_ref[0, i * SPAN:hi, :] = jnp.broadcast_to(m + jnp.log(den), (SPAN, HEAD_DIM))


def _band_attention(proj, cos, sin, qg, kg, l, *, B, S, H, gi, col0):
    win, dil = ATT_GROUPS[gi]
    NP = proj.shape[1]
    L = S // dil
    assert win // dil == SPAN and L % SPAN == 0
    nb = NP // LANES
    G = N_GROUPS
    pv = proj.reshape(B, L, dil * NP)
    cosv = cos.reshape(L, dil * HEAD_DIM)
    sinv = sin.reshape(L, dil * HEAD_DIM)
    cb = col0 // LANES

    def qkv_spec(c):
        return pl.BlockSpec((1, L, HEAD_DIM),
                            lambda b, r, h: (b, 0, r * nb + cb + (c * G + gi) * H + h))

    tab_spec = pl.BlockSpec((L, HEAD_DIM), lambda b, r, h: (0, r))
    g_spec = pl.BlockSpec((None, 1, HEAD_DIM), lambda b, r, h: (l, 0, 0))
    out_spec = pl.BlockSpec((1, L, HEAD_DIM), lambda b, r, h: (b, 0, r * H + h))
    o, lse, knew = pl.pallas_call(
        functools.partial(_band_attn_kernel, L=L),
        grid=(B, dil, H),
        in_specs=[qkv_spec(0), qkv_spec(1), qkv_spec(2), tab_spec, tab_spec, g_spec, g_spec],
        out_specs=[out_spec, out_spec,
                   pl.BlockSpec((1, SPAN, HEAD_DIM), lambda b, r, h: (b, 0, r * H + h))],
        out_shape=[jax.ShapeDtypeStruct((B, L, dil * H * HEAD_DIM), F32),
                   jax.ShapeDtypeStruct((B, L, dil * H * HEAD_DIM), F32),
                   jax.ShapeDtypeStruct((B, SPAN, dil * H * HEAD_DIM), F32)],
        compiler_params=_cparams(("parallel", "parallel", "parallel")),
        name=f"band_attn_g{gi}",
    )(pv, pv, pv, cosv, sinv, qg.reshape(-1, 1, HEAD_DIM), kg.reshape(-1, 1, HEAD_DIM))
    W = H * HEAD_DIM
    return o.reshape(B * S, W), lse.reshape(B * S, W), knew.reshape(B, SPAN * dil, H, HEAD_DIM)


def _combine_kernel(o0, o1, o2, l0, l1, l2, out_ref):
    a0, a1, a2 = l0[...], l1[...], l2[...]
    mx = jnp.maximum(jnp.maximum(a0, a1), a2)
    e0, e1, e2 = jnp.exp(a0 - mx), jnp.exp(a1 - mx), jnp.exp(a2 - mx)
    num = e0 * o0[...] + e1 * o1[...] + e2 * o2[...]
    out_ref[...] = (num / (e0 + e1 + e2)).astype(out_ref.dtype)


def _combine_groups(os_, lses):
    M, W = os_[0].shape
    tm = _pick(M, (512, 256, 128, 64, 32, 16, 8))
    spec = pl.BlockSpec((tm, W), lambda i: (i, 0))
    return pl.pallas_call(
        _combine_kernel,
        grid=(M // tm,),
        in_specs=[spec] * 6,
        out_specs=spec,
        out_shape=jax.ShapeDtypeStruct((M, W), BF16),
        compiler_params=_cparams(("parallel",)),
        name="combine_groups",
    )(*os_, *lses)


def _cached_attn_kernel(p_ref, c0_ref, c1_ref, c2_ref, cos_ref, sin_ref, qg_ref, kg_ref,
                        att_ref, knew_ref, *, H, col0):
    cos = cos_ref[...]
    sin = sin_ref[...]
    scale = HEAD_DIM ** -0.5
    G = N_GROUPS
    W = H * HEAD_DIM
    caches = (c0_ref, c1_ref, c2_ref)
    for h in range(H):
        ms, dens, outs = [], [], []
        for gi in range(G):
            def col(c):
                o = col0 + ((c * G + gi) * H + h) * HEAD_DIM
                return p_ref[0, :, o:o + HEAD_DIM]
            q = _norm_rope(col(0), qg_ref[...], cos, sin)
            k = _norm_rope(col(1), kg_ref[...], cos, sin)
            v = col(2)
            knew_ref[0, :, (gi * H + h) * HEAD_DIM:(gi * H + h + 1) * HEAD_DIM] = k
            kc = caches[gi][0, 0, :, h * HEAD_DIM:(h + 1) * HEAD_DIM]
            vc = caches[gi][0, 0, :, W + h * HEAD_DIM:W + (h + 1) * HEAD_DIM]
            q8 = jnp.broadcast_to(q, (8, HEAD_DIM)).astype(BF16)
            s_c = lax.dot_general(q8, kc.astype(BF16), _NT, preferred_element_type=F32)[0:1] * scale
            s_n = jnp.sum(q.astype(BF16).astype(F32) * k.astype(BF16).astype(F32),
                          axis=-1, keepdims=True) * scale
            m = jnp.maximum(jnp.max(s_c, axis=-1, keepdims=True), s_n)
            p_c = jnp.exp(s_c - m)
            p_n = jnp.exp(s_n - m)
            den = jnp.sum(p_c, axis=-1, keepdims=True) + p_n
            p8 = jnp.broadcast_to(p_c, (8, SPAN)).astype(BF16)
            o = jnp.dot(p8, vc.astype(BF16), preferred_element_type=F32)[0:1]
            o = o + p_n.astype(BF16).astype(F32) * v.astype(BF16).astype(F32)
            ms.append(m)
            dens.append(den)
            outs.append(o)
        mx = jnp.maximum(jnp.maximum(ms[0], ms[1]), ms[2])
        num = jnp.zeros((1, HEAD_DIM), F32)
        tot = jnp.zeros((1, 1), F32)
        for gi in range(G):
            e = jnp.exp(ms[gi] - mx)
            num = num + e * outs[gi]
            tot = tot + e * dens[gi]
        att_ref[0, :, h * HEAD_DIM:(h + 1) * HEAD_DIM] = (num / tot).astype(att_ref.dtype)


def _cached_attention(proj_s, caches, cos, sin, qg, kg, l, *, B, H, col0):
    NP = proj_s.shape[-1]
    W = H * HEAD_DIM
    views, specs = [], []
    for gi, (win, dil) in enumerate(ATT_GROUPS):
        c = caches[gi]
        Lb = c.shape[2]
        assert Lb == win and Lb // dil == SPAN
        views.append(c.reshape(c.shape[0], B, SPAN, dil * 2 * W))
        specs.append(pl.BlockSpec((1, 1, SPAN, 2 * W), lambda b: (l, b, 0, 0)))
    tab = pl.BlockSpec((1, HEAD_DIM), lambda b: (0, 0))
    gsp = pl.BlockSpec((None, 1, HEAD_DIM), lambda b: (l, 0, 0))
    return pl.pallas_call(
        functools.partial(_cached_attn_kernel, H=H, col0=col0),
        grid=(B,),
        in_specs=[pl.BlockSpec((1, 1, NP), lambda b: (b, 0, 0))] + specs + [tab, tab, gsp, gsp],
        out_specs=[pl.BlockSpec((1, 1, W), lambda b: (b, 0, 0)),
                   pl.BlockSpec((1, 1, N_GROUPS * W), lambda b: (b, 0, 0))],
        out_shape=[jax.ShapeDtypeStruct((B, 1, W), F32),
                   jax.ShapeDtypeStruct((B, 1, N_GROUPS * W), F32)],
        compiler_params=_cparams(("parallel",)),
        name="cached_attn",
    )(proj_s, *views, cos, sin, qg.reshape(-1, 1, HEAD_DIM), kg.reshape(-1, 1, HEAD_DIM))


def _head_sum(x, jmat):
    return _mmp(_split(x, 2), [jmat])


def _rwkv_prep_kernel(z_ref, zp_ref, mu_ref, w0_ref, a0_ref, kk_ref, ka_ref,
                      du_ref, iu_ref, gu_ref,
                      r_ref, k_ref, v_ref, ap_ref, b_ref, ld_ref, g_ref,
                      *, RW, single_step, tiles_per_seq):
    z = z_ref[...]
    if single_step:
        zp = zp_ref[...]
    else:
        first = pl.program_id(0) % tiles_per_seq == 0
        prev_row = jnp.where(first, 0.0, zp_ref[7:8, :])
        rows = lax.broadcasted_iota(jnp.int32, z.shape, 0)
        zp = jnp.where(rows == 0, prev_row, pltpu.roll(z, 1, 0))
    zs = z + mu_ref[...] * (zp - z)
    r = zs[:, 0:RW]
    k = zs[:, RW:2 * RW]
    v = zs[:, 2 * RW:3 * RW]
    o = 3 * RW
    w_lo = zs[:, o:o + LORA_W]
    a_lo = zs[:, o + LORA_W:o + LORA_W + LORA_A]
    g_lo = zs[:, o + LORA_W + LORA_A:o + LORA_W + LORA_A + LORA_G_PAD]

    def lora(x, u_ref):
        return _mmp(_split(x, 2), [u_ref[0], u_ref[1]])

    xw = w0_ref[...] + lora(jnp.tanh(w_lo), du_ref)
    ld_ref[...] = -math.exp(-0.5) * jax.nn.sigmoid(xw)
    a = jax.nn.sigmoid(a0_ref[...] + lora(a_lo, iu_ref))
    g_ref[...] = lora(jax.nn.sigmoid(g_lo), gu_ref)
    r_ref[...] = r
    v_ref[...] = v
    k_ref[...] = k * (1.0 + (a - 1.0) * ka_ref[...])
    ri = lax.broadcasted_iota(jnp.int32, (LANES, LANES), 0) // RWKV_HEAD
    ci = lax.broadcasted_iota(jnp.int32, (LANES, LANES), 1) // RWKV_HEAD
    jmat = jnp.where(ri == ci, 1.0, 0.0).astype(BF16)
    for t in range(RW // LANES):
        sl = slice(t * LANES, (t + 1) * LANES)
        kk = k[:, sl] * kk_ref[:, sl]
        ss = _head_sum(kk * kk, jmat)
        kk = kk * lax.rsqrt(jnp.maximum(ss, 1e-24))
        ap_ref[:, sl] = -kk
        b_ref[:, sl] = kk * a[:, sl]


def _rwkv_prep(proj, zprev, l, wts, *, rows_per_seq, single_step):
    M = proj.shape[0]
    RW = wts["decay_w0"].shape[-1]
    ZW = wts["mu_p"].shape[-1]
    tm = 8 if single_step else _pick(rows_per_seq, (128, 64, 32, 16, 8))
    z_spec = pl.BlockSpec((tm, ZW), lambda i: (i, 0))
    if single_step:
        zp_arr, zp_spec = zprev, pl.BlockSpec((tm, ZW), lambda i: (i, 0))
    else:
        zp_arr = proj
        zp_spec = pl.BlockSpec((8, ZW), lambda i: (jnp.maximum(i * (tm // 8) - 1, 0), 0))

    def vec(n):
        return pl.BlockSpec((None, 1, n), lambda i: (l, 0, 0))

    def up(kdim):
        return pl.BlockSpec((None, 2, kdim, RW), lambda i: (l, 0, 0, 0))

    o_spec = pl.BlockSpec((tm, RW), lambda i: (i, 0))
    outs = pl.pallas_call(
        functools.partial(_rwkv_prep_kernel, RW=RW, single_step=single_step,
                          tiles_per_seq=max(rows_per_seq // tm, 1)),
        grid=(M // tm,),
        in_specs=[z_spec, zp_spec, vec(ZW), vec(RW), vec(RW), vec(RW), vec(RW),
                  up(LORA_W), up(LORA_A), up(LORA_G_PAD)],
        out_specs=[o_spec] * 7,
        out_shape=[jax.ShapeDtypeStruct((M, RW), F32)] * 7,
        compiler_params=_cparams(("parallel",)),
        name="rwkv_prep",
    )(proj, zp_arr, wts["mu_p"], wts["decay_w0"], wts["iclr_a0"], wts["key_k"], wts["key_a"],
      wts["decay_up2"], wts["iclr_up2"], wts["gate_up2"])
    return outs


WKV_PARTS = 2


def _wkv_pair(r, k, v, ap, b, ld, wt, consts):
    C = CHUNK
    tri, m2, strict, incl, eye = consts
    np_ = WKV_PARTS
    lc = _mmp([tri], _split(ld, 3))
    lt = lc[C - 1:C, :]
    e_pos = jnp.exp(lc)
    e_neg = jnp.exp(-lc)
    e_end = jnp.exp(lt - lc)
    at = ap * jnp.exp(lc - ld)
    rt = r * e_pos
    bt = b * e_neg
    kt = k * e_neg
    be = b * e_end
    ke = k * e_end

    def stack(x):
        return jnp.where(m2, jnp.concatenate([x, x], axis=0), 0.0)

    at2, rt2, bt2, kt2, v2 = (_split(stack(x), np_) for x in (at, rt, bt, kt, v))
    be2, ke2 = stack(be), stack(ke)
    n_ab = jnp.where(strict, _mmp(at2, bt2, _NT), 0.0)
    a_ak = jnp.where(strict, _mmp(at2, kt2, _NT), 0.0)
    a_rb = jnp.where(incl, _mmp(rt2, bt2, _NT), 0.0)
    a_rk = jnp.where(incl, _mmp(rt2, kt2, _NT), 0.0)
    x = eye + n_ab
    p = n_ab
    for _ in range(int(math.log2(C)) - 1):
        ps = _split(p, np_)
        p = _mmp(ps, ps)
        x = x + _mmp(_split(p, np_), _split(x, np_))
    wts = _split(wt, np_)
    rhs_u = _mmp(at2, wts, _NT) + _mmp(_split(a_ak, np_), v2)
    u2 = _mmp(_split(x, np_), _split(rhs_u, np_))
    u2s = _split(u2, np_)
    y2 = _mmp(rt2, wts, _NT) + _mmp(_split(a_rb, np_), u2s) + _mmp(_split(a_rk, np_), v2)
    y = y2[0:C] + y2[C:2 * C]
    wt_new = wt * jnp.exp(lt) + _mmp(u2s, _split(be2, np_), _TN) + _mmp(v2, _split(ke2, np_), _TN)
    return y, wt_new


def _wkv_kernel(r_ref, k_ref, v_ref, ap_ref, b_ref, ld_ref, g_ref, s0_ref,
                brk_ref, lg_ref, lb_ref, y_ref, sout_ref, wt_scr, *, n_pairs, n_chunks):
    C = CHUNK
    c = pl.program_id(2)
    HD = RWKV_HEAD

    @pl.when(c == 0)
    def _():
        wt_scr[...] = jnp.zeros_like(wt_scr)
        for pi in range(n_pairs):
            wt_scr[pi, 0:HD, 0:HD] = s0_ref[0, 2 * pi]
            wt_scr[pi, HD:2 * HD, HD:2 * HD] = s0_ref[0, 2 * pi + 1]

    ri = lax.broadcasted_iota(jnp.int32, (2 * C, LANES), 0)
    ci = lax.broadcasted_iota(jnp.int32, (2 * C, LANES), 1)
    tri_r = lax.broadcasted_iota(jnp.int32, (C, C), 0)
    tri_c = lax.broadcasted_iota(jnp.int32, (C, C), 1)
    tri = jnp.where(tri_c <= tri_r, 1.0, 0.0).astype(BF16)
    m2 = (ri // C) == (ci // HD)
    same = (ri // C) == (ci // C)
    strict = same & ((ci % C) < (ri % C))
    incl = same & ((ci % C) <= (ri % C))
    eye = jnp.where(ri == ci, 1.0, 0.0)
    jmat = jnp.where((ri // HD) == (ci // HD), 1.0, 0.0).astype(BF16)
    consts = (tri, m2, strict, incl, eye)
    inv = 1.0 / HD
    for pi in range(n_pairs):
        sl = slice(pi * LANES, (pi + 1) * LANES)
        r, k, v = r_ref[0, :, sl], k_ref[0, :, sl], v_ref[0, :, sl]
        y, wt_new = _wkv_pair(r, k, v, ap_ref[0, :, sl], b_ref[0, :, sl], ld_ref[0, :, sl],
                              wt_scr[pi], consts)
        wt_scr[pi] = wt_new
        mu = _head_sum(y, jmat) * inv
        d = y - mu
        var = _head_sum(d * d, jmat) * inv
        yn = d * lax.rsqrt(var + GN_EPS) * lg_ref[:, sl] + lb_ref[:, sl]
        bonus = _head_sum(r * k * brk_ref[:, sl], jmat) * v
        y_ref[0, :, sl] = ((yn + bonus) * g_ref[0, :, sl]).astype(y_ref.dtype)

    @pl.when(c == n_chunks - 1)
    def _():
        for pi in range(n_pairs):
            sout_ref[0, 2 * pi] = wt_scr[pi, 0:HD, 0:HD]
            sout_ref[0, 2 * pi + 1] = wt_scr[pi, HD:2 * HD, HD:2 * HD]


def _wkv(r, k, v, ap, b, ld, g, s0, l, wts, *, B, T):
    RW = r.shape[-1]
    NH = RW // RWKV_HEAD
    WT = _pick(RW, (512, 256, 128))
    n_pairs = WT // LANES
    n_chunks = T // CHUNK
    x_spec = pl.BlockSpec((1, CHUNK, WT), lambda bb, j, c: (bb, c, j))
    s_spec = pl.BlockSpec((1, 2 * n_pairs, RWKV_HEAD, RWKV_HEAD), lambda bb, j, c: (bb, j, 0, 0))
    p_spec = pl.BlockSpec((None, 1, WT), lambda bb, j, c: (l, 0, j))
    sh = lambda t: t.reshape(B, T, RW)
    y, s_new = pl.pallas_call(
        functools.partial(_wkv_kernel, n_pairs=n_pairs, n_chunks=n_chunks),
        grid=(B, RW // WT, n_chunks),
        in_specs=[x_spec] * 7 + [s_spec, p_spec, p_spec, p_spec],
        out_specs=[x_spec, s_spec],
        out_shape=[jax.ShapeDtypeStruct((B, T, RW), BF16),
                   jax.ShapeDtypeStruct((B, NH, RWKV_HEAD, RWKV_HEAD), F32)],
        scratch_shapes=[pltpu.VMEM((n_pairs, LANES, LANES), F32)],
        compiler_params=_cparams(("parallel", "parallel", "arbitrary")),
        name="wkv_chunk",
    )(sh(r), sh(k), sh(v), sh(ap), sh(b), sh(ld), sh(g), s0,
      wts["bonus_rk"], wts["lnx_g"], wts["lnx_b"])
    return y.reshape(B * T, RW), s_new


def _layer(x, l, wb, wts, rope, *, B, T, caches, wkv0, shift0):
    D = x.shape[1]
    M = B * T
    H = wb["w_att_o"].shape[0] // HEAD_DIM
    W = H * HEAD_DIM
    ZW = wts["mu_p"].shape[-1]
    col_att = ZW
    col_gate = ZW + 3 * N_GROUPS * W
    cos, sin = rope
    xn = _rmsnorm(x, wts["norm1_g"], l)
    proj = _matmul(xn, wb["w_in"], out_dtype=F32, name="in_proj")
    if caches is None:
        os_, lses, knews = [], [], []
        for gi in range(N_GROUPS):
            o, lse, kn = _band_attention(proj, cos, sin, wts["q_norm_g"], wts["k_norm_g"], l,
                                         B=B, S=T, H=H, gi=gi, col0=col_att)
            os_.append(o)
            lses.append(lse)
            knews.append(kn)
        att = _combine_groups(os_, lses)
    else:
        att, kn = _cached_attention(proj[:M].reshape(B, 1, -1), caches, cos, sin,
                                    wts["q_norm_g"], wts["k_norm_g"], l, B=B, H=H, col0=col_att)
        att = jnp.pad(att.reshape(B, W).astype(BF16), ((0, x.shape[0] - M), (0, 0)))
        kn = kn.reshape(B, 1, N_GROUPS, H, HEAD_DIM)
        knews = [kn[:, :, gi] for gi in range(N_GROUPS)]
    if caches is None:
        r, k, v, ap, b, ld, g = _rwkv_prep(proj, None, l, wts, rows_per_seq=T, single_step=False)
        yr, wkv_new = _wkv(r, k, v, ap, b, ld, g, wkv0, l, wts, B=B, T=T)
    else:
        zprev = jnp.pad(shift0, ((0, x.shape[0] - M), (0, ZW - shift0.shape[1])))
        outs = _rwkv_prep(proj, zprev, l, wts, rows_per_seq=1, single_step=True)
        padded = [jnp.pad(t[:M].reshape(B, 1, -1), ((0, 0), (0, CHUNK - 1), (0, 0))).reshape(B * CHUNK, -1)
                  for t in outs]
        yr, wkv_new = _wkv(*padded, wkv0, l, wts, B=B, T=CHUNK)
        yr = jnp.pad(yr.reshape(B, CHUNK, -1)[:, 0], ((0, x.shape[0] - M), (0, 0)))
    t1 = _matmul(att, wb["w_att_o"], out_dtype=F32, epilogue=_ep_gate,
                 extras=[(proj, col_gate)], name="att_out")
    merged = _matmul(yr, wb["w_rwkv_o"], out_dtype=BF16, epilogue=_ep_gate_add,
                     extras=[(proj, col_gate + D), (t1, 0)], name="rwkv_out_merge")
    x1 = _matmul(merged, wb["w_out"], out_dtype=F32, epilogue=_ep_residual,
                 extras=[(x, 0)], tn=_pick(D, (512, 256, 128)), name="w_out")
    hn = _rmsnorm(x1, wts["norm2_g"], l)
    hh = _matmul(hn, wb["w_ffn_up"], out_dtype=BF16, epilogue=_ep_relu2, name="ffn_up")
    x2 = _matmul(hh, wb["w_ffn_down"], out_dtype=F32, epilogue=_ep_residual,
                 extras=[(x1, 0)], tk=_pick(hh.shape[1], (2048, 1024, 512, 256, 128)), name="ffn_down")
    return x2, knews, proj, wkv_new


def _rope_tables(pos):
    half = HEAD_DIM // 2
    inv_freq = ROPE_THETA ** (-jnp.arange(half, dtype=F32) / half)
    ang = pos.astype(F32)[:, None] * inv_freq[None, :]
    cos, sin = jnp.cos(ang), jnp.sin(ang)
    return jnp.concatenate([cos, cos], axis=-1), jnp.concatenate([-sin, sin], axis=-1)


def _split_hi_lo(w):
    hi = w.astype(BF16)
    lo = (w - hi.astype(F32)).astype(BF16)
    return jnp.stack([hi, lo], axis=1)


def kernel(x_prompt, x_sample, cache_kv_w128, cache_kv_w512, cache_kv_w2048, state_wkv, state_shift,
           norm1_g, w_in, q_norm_g, k_norm_g, shift_mu, decay_w0, decay_up, iclr_a0, iclr_up, gate_up,
           key_k, key_a, bonus_rk, lnx_g, lnx_b, w_att_o, w_rwkv_o, w_out, norm2_g, w_ffn_up, w_ffn_down):
    B, S, D = x_prompt.shape
    Bs, Ts, _ = x_sample.shape
    assert Ts == 1
    DEPTH = w_in.shape[0]
    RW = decay_w0.shape[-1]
    NH = RW // RWKV_HEAD
    W = w_att_o.shape[1]
    H = W // HEAD_DIM
    att_cols = 3 * N_GROUPS * W
    rwkv_cols = 3 * RW + LORA_W + LORA_A + LORA_G
    ZW = -(-(3 * RW + LORA_W + LORA_A + LORA_G_PAD) // 1024) * 1024
    Ms = 16

    vec = lambda t: t.reshape(DEPTH, 1, -1)
    wts = {
        "norm1_g": norm1_g, "norm2_g": norm2_g, "q_norm_g": q_norm_g, "k_norm_g": k_norm_g,
        "mu_p": vec(jnp.pad(shift_mu, ((0, 0), (0, ZW - rwkv_cols)))),
        "decay_w0": vec(decay_w0), "iclr_a0": vec(iclr_a0), "key_k": vec(key_k), "key_a": vec(key_a),
        "bonus_rk": vec(bonus_rk), "lnx_g": vec(lnx_g), "lnx_b": vec(lnx_b),
        "decay_up2": _split_hi_lo(decay_up), "iclr_up2": _split_hi_lo(iclr_up),
        "gate_up2": _split_hi_lo(jnp.pad(gate_up, ((0, 0), (0, LORA_G_PAD - LORA_G), (0, 0)))),
    }
    pos_p = jnp.arange(S, dtype=jnp.int32)
    rope_p = _rope_tables(pos_p)
    caches = (cache_kv_w128, cache_kv_w512, cache_kv_w2048)

    yp = x_prompt.reshape(B * S, D)
    ys = jnp.pad(x_sample.reshape(Bs, D), ((0, Ms - Bs), (0, 0)))
    zeros_wkv = jnp.zeros((B, NH, RWKV_HEAD, RWKV_HEAD), F32)
    outs = {k_: [] for k_ in ("pk0", "pk1", "pk2", "pwkv", "pshift", "sk0", "sk1", "sk2", "swkv", "sshift")}
    rope_s = _rope_tables(PAST_LEN + jnp.arange(Ts, dtype=jnp.int32))
    for l in range(DEPTH):
        wl = w_in[l]
        w_in_p = jnp.concatenate(
            [wl[:, att_cols:att_cols + rwkv_cols].astype(BF16),
             jnp.zeros((D, ZW - rwkv_cols), BF16),
             wl[:, :att_cols].astype(BF16),
             wl[:, att_cols + rwkv_cols:].astype(BF16)], axis=1)
        wb = {"w_in": w_in_p, "w_att_o": w_att_o[l].astype(BF16), "w_rwkv_o": w_rwkv_o[l].astype(BF16),
              "w_out": w_out[l].astype(BF16), "w_ffn_up": w_ffn_up[l].astype(BF16),
              "w_ffn_down": w_ffn_down[l].astype(BF16)}
        yp, kn_p, proj_p, wkv_p = _layer(yp, l, wb, wts, rope_p, B=B, T=S, caches=None,
                                         wkv0=zeros_wkv, shift0=None)
        ys, kn_s, proj_s, wkv_s = _layer(ys, l, wb, wts, rope_s, B=Bs, T=1, caches=caches,
                                         wkv0=state_wkv[l], shift0=state_shift[l])
        pp = proj_p.reshape(B, S, -1)
        for gi, (win, dil) in enumerate(ATT_GROUPS):
            keep = min(win, S)
            c0 = ZW + (2 * N_GROUPS + gi) * W
            v_new = pp[:, S - keep:, c0:c0 + W].reshape(B, keep, H, HEAD_DIM)
            outs[f"pk{gi}"].append(jnp.stack([kn_p[gi], v_new], axis=2))
            v_s = proj_s[:Bs, c0:c0 + W].reshape(Bs, 1, H, HEAD_DIM)
            new_row = jnp.stack([kn_s[gi], v_s], axis=2)
            outs[f"sk{gi}"].append(jnp.concatenate([caches[gi][l][:, 1:], new_row], axis=1))
        outs["pwkv"].append(wkv_p)
        outs["swkv"].append(wkv_s)
        outs["pshift"].append(pp[:, -1, :rwkv_cols])
        outs["sshift"].append(proj_s[:Bs, :rwkv_cols])
    st = lambda n: jnp.stack(outs[n])
    return (yp.reshape(B, S, D), ys[:Bs].reshape(Bs, 1, D),
            st("pk0"), st("pk1"), st("pk2"), st("pwkv"), st("pshift"),
            st("sk0"), st("sk1"), st("sk2"), st("swkv"), st("sshift"))
```

```python
import functools
import math

import jax
import jax.numpy as jnp
from jax import lax
from jax.experimental import pallas as pl
from jax.experimental.pallas import tpu as pltpu

F32 = jnp.float32
BF16 = jnp.bfloat16

N_GROUPS = 3
ATT_GROUPS = ((128, 1), (512, 4), (2048, 16))
HEAD_DIM = 128
RWKV_HEAD = 64
LORA_W = 128
LORA_A = 128
LORA_G = 480
ROPE_THETA = 10000.0
PAST_LEN = 8192
NEG_INF = -1e30
GN_EPS = 64e-5
NORM_EPS = 1e-6

LANES = 128
VMEM_LIMIT_BYTES = 56 * 1024 * 1024

CHUNK = 64
LORA_G_PAD = 512
SPAN = 128


def _cparams(sem):
    return pltpu.CompilerParams(dimension_semantics=sem, vmem_limit_bytes=VMEM_LIMIT_BYTES)


def _pick(n, prefs):
    for p in prefs:
        if n % p == 0:
            return p
    return n


def _split(x, n):
    parts, rem = [], x
    for i in range(n):
        p = rem.astype(BF16)
        parts.append(p)
        if i + 1 < n:
            rem = rem - p.astype(F32)
    return parts


_NN = (((1,), (0,)), ((), ()))
_NT = (((1,), (1,)), ((), ()))
_TN = (((0,), (0,)), ((), ()))


def _mmp(a_parts, b_parts, dims=_NN):
    order = max(len(a_parts), len(b_parts))
    acc = None
    for i, a in enumerate(a_parts):
        for j, b in enumerate(b_parts):
            if i + j < order:
                t = lax.dot_general(a, b, dims, preferred_element_type=F32)
                acc = t if acc is None else acc + t
    return acc


def _rms_kernel(x_ref, g_ref, o_ref):
    x = x_ref[...]
    ms = jnp.mean(x * x, axis=-1, keepdims=True)
    o_ref[...] = (x * lax.rsqrt(ms + NORM_EPS) * g_ref[...]).astype(o_ref.dtype)


def _rmsnorm(x, g, l):
    M, D = x.shape
    tm = _pick(M, (256, 128, 64, 32, 16, 8))
    return pl.pallas_call(
        _rms_kernel,
        grid=(M // tm,),
        in_specs=[pl.BlockSpec((tm, D), lambda i: (i, 0)),
                  pl.BlockSpec((None, 1, D), lambda i: (l, 0, 0))],
        out_specs=pl.BlockSpec((tm, D), lambda i: (i, 0)),
        out_shape=jax.ShapeDtypeStruct((M, D), BF16),
        compiler_params=_cparams(("parallel",)),
        name="rmsnorm",
    )(x, g.reshape(g.shape[0], 1, D))


def _ep_none(acc):
    return acc


def _ep_relu2(acc):
    return jnp.square(jnp.maximum(acc, 0.0))


def _ep_gate(acc, g):
    return jax.nn.sigmoid(g) * acc


def _ep_gate_add(acc, g, t):
    return t + jax.nn.sigmoid(g) * acc


def _ep_residual(acc, x):
    return x + acc


def _mm_kernel(*refs, nk, n_extra, epilogue):
    a_ref, w_ref = refs[0], refs[1]
    extra = refs[2:2 + n_extra]
    o_ref = refs[2 + n_extra]
    if nk == 1:
        acc = jnp.dot(a_ref[...], w_ref[...], preferred_element_type=F32)
        o_ref[...] = epilogue(acc, *[e[...] for e in extra]).astype(o_ref.dtype)
        return
    acc_ref = refs[3 + n_extra]
    k = pl.program_id(2)

    @pl.when(k == 0)
    def _():
        acc_ref[...] = jnp.zeros_like(acc_ref)

    acc_ref[...] += jnp.dot(a_ref[...], w_ref[...], preferred_element_type=F32)

    @pl.when(k == nk - 1)
    def _():
        o_ref[...] = epilogue(acc_ref[...], *[e[...] for e in extra]).astype(o_ref.dtype)


def _matmul(a, w, l, *, out_dtype, n_cols=None, w_col0=0, epilogue=_ep_none, extras=(),
            tm=None, tn=None, tk=None, name="matmul"):
    M, K = a.shape
    N = n_cols or w.shape[2]
    tm = tm or _pick(M, (1024, 512, 256, 128, 64, 32, 16))
    tn = tn or _pick(math.gcd(N, w_col0) if w_col0 else N, (1024, 512, 256, 128))
    tk = tk or _pick(K, (4096, 2048, 1024, 512, 256, 128))
    nk = K // tk
    assert w_col0 % tn == 0 and N % tn == 0 and M % tm == 0
    in_specs = [pl.BlockSpec((tm, tk), lambda i, j, k: (i, k)),
                pl.BlockSpec((None, tk, tn), lambda i, j, k, _o=w_col0 // tn: (l, k, j + _o))]
    args = [a, w]
    for arr, off in extras:
        assert off % tn == 0
        in_specs.append(pl.BlockSpec((tm, tn), lambda i, j, k, _o=off // tn: (i, j + _o)))
        args.append(arr)
    scratch = [] if nk == 1 else [pltpu.VMEM((tm, tn), F32)]
    return pl.pallas_call(
        functools.partial(_mm_kernel, nk=nk, n_extra=len(extras), epilogue=epilogue),
        grid=(M // tm, N // tn, nk),
        in_specs=in_specs,
        out_specs=pl.BlockSpec((tm, tn), lambda i, j, k: (i, j)),
        out_shape=jax.ShapeDtypeStruct((M, N), out_dtype),
        scratch_shapes=scratch,
        compiler_params=_cparams(("parallel", "parallel", "arbitrary")),
        name=name,
    )(*args)


def _norm_rope(x, g, cos, sin):
    ms = jnp.mean(x * x, axis=-1, keepdims=True)
    y = x * lax.rsqrt(ms + NORM_EPS) * g
    return y * cos + pltpu.roll(y, HEAD_DIM // 2, 1) * sin


ROW_CHUNK = 256


def _band_attn_kernel(q0, k0, v0, q1, k1, v1, q2, k2, v2, cos_ref, sin_ref, qg_ref, kg_ref,
                      att_ref, kn0, kn1, kn2, vn0, vn1, vn2, qs, ks, os_, ls, *, S):
    scale = HEAD_DIM ** -0.5
    qkv = ((q0, k0, v0, kn0, vn0), (q1, k1, v1, kn1, vn1), (q2, k2, v2, kn2, vn2))
    for gi, (q_ref, k_ref, v_ref, kn_ref, vn_ref) in enumerate(qkv):
        win, dil = ATT_GROUPS[gi]
        L = S // dil
        for c in range(S // ROW_CHUNK):
            sl = slice(c * ROW_CHUNK, (c + 1) * ROW_CHUNK)
            cos, sin = cos_ref[sl, :], sin_ref[sl, :]
            qs[sl, :] = _norm_rope(q_ref[0, sl, :], qg_ref[...], cos, sin)
            ks[sl, :] = _norm_rope(k_ref[0, sl, :], kg_ref[...], cos, sin)
        kn_ref[0] = ks[S - win:, :]
        vn_ref[0] = v_ref[0, S - win:, :]

        def rows(r, start, n, dil=dil):
            return pl.ds(start, n) if dil == 1 else pl.ds(r + start * dil, n, stride=dil)

        for r in range(dil):
            for i in range(L // SPAN):
                lo = max(i - 1, 0) * SPAN
                nkeys = (i + 1) * SPAN - lo
                qb = qs[rows(r, i * SPAN, SPAN), :].astype(BF16)
                kb = ks[rows(r, lo, nkeys), :].astype(BF16)
                vb = v_ref[0, rows(r, lo, nkeys), :].astype(BF16)
                s = lax.dot_general(qb, kb, _NT, preferred_element_type=F32) * scale
                qi = lax.broadcasted_iota(jnp.int32, (SPAN, nkeys), 0) + (i * SPAN - lo)
                ki = lax.broadcasted_iota(jnp.int32, (SPAN, nkeys), 1)
                s = jnp.where((ki <= qi) & (ki >= qi - SPAN), s, NEG_INF)
                m = jnp.max(s, axis=-1, keepdims=True)
                p = jnp.exp(s - m)
                den = jnp.sum(p, axis=-1, keepdims=True)
                o = jnp.dot(p.astype(BF16), vb, preferred_element_type=F32)
                os_[gi, rows(r, i * SPAN, SPAN), :] = o / den
                ls[gi, rows(r, i * SPAN, SPAN), :] = jnp.broadcast_to(m + jnp.log(den), (SPAN, HEAD_DIM))
    for c in range(S // ROW_CHUNK):
        sl = slice(c * ROW_CHUNK, (c + 1) * ROW_CHUNK)
        a0, a1, a2 = ls[0, sl, :], ls[1, sl, :], ls[2, sl, :]
        mx = jnp.maximum(jnp.maximum(a0, a1), a2)
        e0, e1, e2 = jnp.exp(a0 - mx), jnp.exp(a1 - mx), jnp.exp(a2 - mx)
        num = e0 * os_[0, sl, :] + e1 * os_[1, sl, :] + e2 * os_[2, sl, :]
        att_ref[0, sl, :] = (num / (e0 + e1 + e2)).astype(att_ref.dtype)


def _band_attention(proj, cos, sin, qg, kg, l, *, B, S, H):
    NP = proj.shape[1]
    G = N_GROUPS
    W = H * HEAD_DIM
    for win, dil in ATT_GROUPS:
        assert win // dil == SPAN and (S // dil) % SPAN == 0 and S % ROW_CHUNK == 0 and win <= S
    pv = proj.reshape(B, S, NP)
    qkv_specs = [pl.BlockSpec((1, S, HEAD_DIM), lambda b, h, _o=(c * G + gi) * H: (b, 0, _o + h))
                 for gi in range(G) for c in range(3)]
    tab_spec = pl.BlockSpec((S, HEAD_DIM), lambda b, h: (0, 0))
    g_spec = pl.BlockSpec((None, 1, HEAD_DIM), lambda b, h: (l, 0, 0))
    win_specs = [pl.BlockSpec((1, win, HEAD_DIM), lambda b, h: (b, 0, h)) for win, _ in ATT_GROUPS]
    win_shapes = [jax.ShapeDtypeStruct((B, win, W), F32) for win, _ in ATT_GROUPS]
    outs = pl.pallas_call(
        functools.partial(_band_attn_kernel, S=S),
        grid=(B, H),
        in_specs=qkv_specs + [tab_spec, tab_spec, g_spec, g_spec],
        out_specs=[pl.BlockSpec((1, S, HEAD_DIM), lambda b, h: (b, 0, h))] + win_specs + win_specs,
        out_shape=[jax.ShapeDtypeStruct((B, S, W), BF16)] + win_shapes + win_shapes,
        scratch_shapes=[pltpu.VMEM((S, HEAD_DIM), F32), pltpu.VMEM((S, HEAD_DIM), F32),
                        pltpu.VMEM((G, S, HEAD_DIM), F32), pltpu.VMEM((G, S, HEAD_DIM), F32)],
        compiler_params=_cparams(("parallel", "parallel")),
        name="band_attn",
    )(*([pv] * (3 * G)), cos, sin, qg.reshape(-1, 1, HEAD_DIM), kg.reshape(-1, 1, HEAD_DIM))
    att = outs[0].reshape(B * S, W)
    heads = lambda t: t.reshape(B, t.shape[1], H, HEAD_DIM)
    return att, [heads(t) for t in outs[1:1 + G]], [heads(t) for t in outs[1 + G:]]


def _cached_attn_kernel(p_ref, c0_ref, c1_ref, c2_ref, cos_ref, sin_ref, qg_ref, kg_ref,
                        att_ref, knew_ref, *, H):
    cos = cos_ref[...]
    sin = sin_ref[...]
    scale = HEAD_DIM ** -0.5
    G = N_GROUPS
    W = H * HEAD_DIM
    caches = (c0_ref, c1_ref, c2_ref)
    for h in range(H):
        ms, dens, outs = [], [], []
        for gi in range(G):
            def col(c):
                o = ((c * G + gi) * H + h) * HEAD_DIM
                return p_ref[0, :, o:o + HEAD_DIM]
            q = _norm_rope(col(0), qg_ref[...], cos, sin)
            k = _norm_rope(col(1), kg_ref[...], cos, sin)
            v = col(2)
            knew_ref[0, :, (gi * H + h) * HEAD_DIM:(gi * H + h + 1) * HEAD_DIM] = k
            kc = caches[gi][0, 0, :, 0, 0, h, :]
            vc = caches[gi][0, 0, :, 0, 1, h, :]
            q8 = jnp.broadcast_to(q, (8, HEAD_DIM)).astype(BF16)
            s_c = lax.dot_general(q8, kc.astype(BF16), _NT, preferred_element_type=F32)[0:1] * scale
            s_n = jnp.sum(q.astype(BF16).astype(F32) * k.astype(BF16).astype(F32),
                          axis=-1, keepdims=True) * scale
            m = jnp.maximum(jnp.max(s_c, axis=-1, keepdims=True), s_n)
            p_c = jnp.exp(s_c - m)
            p_n = jnp.exp(s_n - m)
            den = jnp.sum(p_c, axis=-1, keepdims=True) + p_n
            p8 = jnp.broadcast_to(p_c, (8, SPAN)).astype(BF16)
            o = jnp.dot(p8, vc.astype(BF16), preferred_element_type=F32)[0:1]
            o = o + p_n.astype(BF16).astype(F32) * v.astype(BF16).astype(F32)
            ms.append(m)
            dens.append(den)
            outs.append(o)
        mx = jnp.maximum(jnp.maximum(ms[0], ms[1]), ms[2])
        num = jnp.zeros((1, HEAD_DIM), F32)
        tot = jnp.zeros((1, 1), F32)
        for gi in range(G):
            e = jnp.exp(ms[gi] - mx)
            num = num + e * outs[gi]
            tot = tot + e * dens[gi]
        att_ref[0, :, h * HEAD_DIM:(h + 1) * HEAD_DIM] = (num / tot).astype(att_ref.dtype)


def _cached_attention(proj_s, caches, cos, sin, qg, kg, l, *, B, H):
    NP = proj_s.shape[-1]
    W = H * HEAD_DIM
    views, specs = [], []
    for gi, (win, dil) in enumerate(ATT_GROUPS):
        c = caches[gi]
        Lb = c.shape[2]
        assert Lb == win and Lb // dil == SPAN
        views.append(c.reshape(c.shape[0], B, SPAN, dil, 2, H, HEAD_DIM))
        specs.append(pl.BlockSpec((1, 1, SPAN, 1, 2, H, HEAD_DIM), lambda b: (l, b, 0, 0, 0, 0, 0)))
    tab = pl.BlockSpec((1, HEAD_DIM), lambda b: (0, 0))
    gsp = pl.BlockSpec((None, 1, HEAD_DIM), lambda b: (l, 0, 0))
    return pl.pallas_call(
        functools.partial(_cached_attn_kernel, H=H),
        grid=(B,),
        in_specs=[pl.BlockSpec((1, 1, NP), lambda b: (b, 0, 0))] + specs + [tab, tab, gsp, gsp],
        out_specs=[pl.BlockSpec((1, 1, W), lambda b: (b, 0, 0)),
                   pl.BlockSpec((1, 1, N_GROUPS * W), lambda b: (b, 0, 0))],
        out_shape=[jax.ShapeDtypeStruct((B, 1, W), F32),
                   jax.ShapeDtypeStruct((B, 1, N_GROUPS * W), F32)],
        compiler_params=_cparams(("parallel",)),
        name="cached_attn",
    )(proj_s, *views, cos, sin, qg.reshape(-1, 1, HEAD_DIM), kg.reshape(-1, 1, HEAD_DIM))


def _shift_append_kernel(c_ref, nxt_ref, new_ref, o_ref, *, R, n_tiles):
    j = pl.program_id(2)
    o_ref[0, 0, 0:R - 1] = c_ref[0, 0, 1:R]
    o_ref[0, 0, R - 1] = jnp.where(j == n_tiles - 1, new_ref[0, 0, 0], nxt_ref[0, 0, 0])


def _shift_append(cache, new_rows):
    DEPTH, B, Lb = cache.shape[:3]
    tail = cache.shape[3:]
    R = _pick(Lb, (256, 128, 64, 32, 16, 8))
    n_tiles = Lb // R
    z = (0,) * len(tail)
    return pl.pallas_call(
        functools.partial(_shift_append_kernel, R=R, n_tiles=n_tiles),
        grid=(DEPTH, B, n_tiles),
        in_specs=[pl.BlockSpec((1, 1, R) + tail, lambda l, b, j: (l, b, j) + z),
                  pl.BlockSpec((1, 1, 1) + tail, lambda l, b, j: (l, b, jnp.minimum((j + 1) * R, Lb - 1)) + z),
                  pl.BlockSpec((1, 1, 1) + tail, lambda l, b, j: (l, b, 0) + z)],
        out_specs=pl.BlockSpec((1, 1, R) + tail, lambda l, b, j: (l, b, j) + z),
        out_shape=jax.ShapeDtypeStruct(cache.shape, cache.dtype),
        compiler_params=_cparams(("parallel", "parallel", "parallel")),
        name="shift_append",
    )(cache, cache, new_rows)


def _head_sum(x, jmat):
    return _mmp(_split(x, 2), [jmat])


def _rwkv_prep_kernel(z_ref, zp_ref, mu_ref, w0_ref, a0_ref, kk_ref, ka_ref,
                      du_ref, iu_ref, gu_ref,
                      r_ref, k_ref, v_ref, ap_ref, b_ref, ld_ref, g_ref,
                      *, RW, single_step, tiles_per_seq):
    z = z_ref[...]
    if single_step:
        zp = zp_ref[...]
    else:
        first = pl.program_id(0) % tiles_per_seq == 0
        prev_row = jnp.where(first, 0.0, zp_ref[7:8, :])
        rows = lax.broadcasted_iota(jnp.int32, z.shape, 0)
        zp = jnp.where(rows == 0, prev_row, pltpu.roll(z, 1, 0))
    zs = z + mu_ref[...] * (zp - z)
    r = zs[:, 0:RW]
    k = zs[:, RW:2 * RW]
    v = zs[:, 2 * RW:3 * RW]
    o = 3 * RW
    w_lo = zs[:, o:o + LORA_W]
    a_lo = zs[:, o + LORA_W:o + LORA_W + LORA_A]
    g_lo = zs[:, o + LORA_W + LORA_A:o + LORA_W + LORA_A + LORA_G_PAD]

    def lora(x, u_ref):
        return _mmp(_split(x, 2), [u_ref[0], u_ref[1]])

    xw = w0_ref[...] + lora(jnp.tanh(w_lo), du_ref)
    ld_ref[...] = -math.exp(-0.5) * jax.nn.sigmoid(xw)
    a = jax.nn.sigmoid(a0_ref[...] + lora(a_lo, iu_ref))
    g_ref[...] = lora(jax.nn.sigmoid(g_lo), gu_ref)
    r_ref[...] = r
    v_ref[...] = v
    k_ref[...] = k * (1.0 + (a - 1.0) * ka_ref[...])
    ri = lax.broadcasted_iota(jnp.int32, (LANES, LANES), 0) // RWKV_HEAD
    ci = lax.broadcasted_iota(jnp.int32, (LANES, LANES), 1) // RWKV_HEAD
    jmat = jnp.where(ri == ci, 1.0, 0.0).astype(BF16)
    for t in range(RW // LANES):
        sl = slice(t * LANES, (t + 1) * LANES)
        kk = k[:, sl] * kk_ref[:, sl]
        ss = _head_sum(kk * kk, jmat)
        kk = kk * lax.rsqrt(jnp.maximum(ss, 1e-24))
        ap_ref[:, sl] = -kk
        b_ref[:, sl] = kk * a[:, sl]


def _rwkv_prep(proj, zprev, l, wts, *, rows_per_seq, single_step):
    M = proj.shape[0]
    RW = wts["decay_w0"].shape[-1]
    ZW = wts["mu_p"].shape[-1]
    tm = 8 if single_step else _pick(rows_per_seq, (128, 64, 32, 16, 8))
    z_spec = pl.BlockSpec((tm, ZW), lambda i: (i, 0))
    if single_step:
        zp_arr, zp_spec = zprev, pl.BlockSpec((tm, ZW), lambda i: (i, 0))
    else:
        zp_arr = proj
        zp_spec = pl.BlockSpec((8, ZW), lambda i: (jnp.maximum(i * (tm // 8) - 1, 0), 0))

    def vec(n):
        return pl.BlockSpec((None, 1, n), lambda i: (l, 0, 0))

    def up(kdim):
        return pl.BlockSpec((None, 2, kdim, RW), lambda i: (l, 0, 0, 0))

    o_spec = pl.BlockSpec((tm, RW), lambda i: (i, 0))
    outs = pl.pallas_call(
        functools.partial(_rwkv_prep_kernel, RW=RW, single_step=single_step,
                          tiles_per_seq=max(rows_per_seq // tm, 1)),
        grid=(M // tm,),
        in_specs=[z_spec, zp_spec, vec(ZW), vec(RW), vec(RW), vec(RW), vec(RW),
                  up(LORA_W), up(LORA_A), up(LORA_G_PAD)],
        out_specs=[o_spec] * 7,
        out_shape=[jax.ShapeDtypeStruct((M, RW), F32)] * 7,
        compiler_params=_cparams(("parallel",)),
        name="rwkv_prep",
    )(proj, zp_arr, wts["mu_p"], wts["decay_w0"], wts["iclr_a0"], wts["key_k"], wts["key_a"],
      wts["decay_up2"], wts["iclr_up2"], wts["gate_up2"])
    return outs


WKV_PARTS = 1


def _mm(a, b, dims=_NN, parts=WKV_PARTS):
    return _mmp(_split(a, parts), _split(b, parts), dims)


def _each(f, *lists):
    return [f(*xs) for xs in zip(*lists)]


def _wkv_pairs(r, k, v, ap, b, ld, wt, consts):
    C = CHUNK
    C2 = 2 * C
    tri, m2, strict, incl, eye = consts
    cat0 = lambda *xs: jnp.concatenate(xs, axis=0)
    lc = _each(lambda x: _mmp([tri], _split(x, 3)), ld)
    lt = [x[C - 1:C, :] for x in lc]

    def stack(x):
        return jnp.where(m2, jnp.concatenate([x, x], axis=0), 0.0)

    ar2 = _each(lambda ap_, r_, lc_, ld_: cat0(stack(ap_ * jnp.exp(lc_ - ld_)), stack(r_ * jnp.exp(lc_))),
                ap, r, lc, ld)
    bk2 = _each(lambda b_, k_, lc_: cat0(stack(b_ * jnp.exp(-lc_)), stack(k_ * jnp.exp(-lc_))), b, k, lc)
    bke = _each(lambda b_, k_, lc_, lt_: cat0(stack(b_ * jnp.exp(lt_ - lc_)), stack(k_ * jnp.exp(lt_ - lc_))),
                b, k, lc, lt)
    v2 = _each(stack, v)
    a4 = _each(lambda x_, y_: _mm(x_, y_, _NT), ar2, bk2)
    n_ab = [jnp.where(strict, a[0:C2, 0:C2], 0.0) for a in a4]
    a_akrk = [cat0(jnp.where(strict, a[0:C2, C2:], 0.0), jnp.where(incl, a[C2:, C2:], 0.0)) for a in a4]
    a_rb = [jnp.where(incl, a[C2:, 0:C2], 0.0) for a in a4]
    sw = _each(lambda x_, w_: _mm(x_, w_, _NT), ar2, wt)
    av = _each(_mm, a_akrk, v2)
    p = _each(_mm, n_ab, n_ab)
    x = [eye + n for n in n_ab]
    n_lvl = int(math.log2(C)) - 1
    for lvl in range(n_lvl):
        if lvl + 1 < n_lvl:
            res = _each(lambda p_, x_: _mm(p_, jnp.concatenate([p_, x_], axis=1)), p, x)
            p = [t[:, 0:C2] for t in res]
            x = _each(lambda x_, t: x_ + t[:, C2:], x, res)
        else:
            x = _each(lambda x_, p_: x_ + _mm(p_, x_), x, p)
    u2 = _each(lambda x_, s_, a_: _mm(x_, s_[0:C2] + a_[0:C2]), x, sw, av)
    y2 = _each(lambda s_, a_, rb_, u_: s_[C2:] + a_[C2:] + _mm(rb_, u_), sw, av, a_rb, u2)
    y = [t[0:C] + t[C:C2] for t in y2]
    wt_new = _each(lambda w_, lt_, u_, v_, e_: w_ * jnp.exp(lt_) + _mm(cat0(u_, v_), e_, _TN),
                   wt, lt, u2, v2, bke)
    return y, wt_new


def _wkv_kernel(r_ref, k_ref, v_ref, ap_ref, b_ref, ld_ref, g_ref, s0_ref,
                brk_ref, lg_ref, lb_ref, y_ref, sout_ref, wt_scr, *, n_pairs, n_chunks):
    C = CHUNK
    c = pl.program_id(2)
    HD = RWKV_HEAD

    @pl.when(c == 0)
    def _():
        wt_scr[...] = jnp.zeros_like(wt_scr)
        for pi in range(n_pairs):
            wt_scr[pi, 0:HD, 0:HD] = s0_ref[0, 2 * pi]
            wt_scr[pi, HD:2 * HD, HD:2 * HD] = s0_ref[0, 2 * pi + 1]

    ri = lax.broadcasted_iota(jnp.int32, (2 * C, LANES), 0)
    ci = lax.broadcasted_iota(jnp.int32, (2 * C, LANES), 1)
    tri_r = lax.broadcasted_iota(jnp.int32, (C, C), 0)
    tri_c = lax.broadcasted_iota(jnp.int32, (C, C), 1)
    tri = jnp.where(tri_c <= tri_r, 1.0, 0.0).astype(BF16)
    m2 = (ri // C) == (ci // HD)
    same = (ri // C) == (ci // C)
    strict = same & ((ci % C) < (ri % C))
    incl = same & ((ci % C) <= (ri % C))
    eye = jnp.where(ri == ci, 1.0, 0.0)
    jmat = jnp.where((ri // HD) == (ci // HD), 1.0, 0.0).astype(BF16)
    consts = (tri, m2, strict, incl, eye)
    inv = 1.0 / HD
    sls = [slice(pi * LANES, (pi + 1) * LANES) for pi in range(n_pairs)]
    col = lambda ref: [ref[0, :, sl] for sl in sls]
    r, k, v = col(r_ref), col(k_ref), col(v_ref)
    y, wt_new = _wkv_pairs(r, k, v, col(ap_ref), col(b_ref), col(ld_ref),
                           [wt_scr[pi] for pi in range(n_pairs)], consts)
    for pi in range(n_pairs):
        wt_scr[pi] = wt_new[pi]
    mu = [_head_sum(t, jmat) * inv for t in y]
    d = _each(lambda y_, m_: y_ - m_, y, mu)
    var = [_head_sum(t * t, jmat) * inv for t in d]
    rk = _each(lambda r_, k_, sl: _head_sum(r_ * k_ * brk_ref[:, sl], jmat), r, k, sls)
    for pi, sl in enumerate(sls):
        yn = d[pi] * lax.rsqrt(var[pi] + GN_EPS) * lg_ref[:, sl] + lb_ref[:, sl]
        y_ref[0, :, sl] = ((yn + rk[pi] * v[pi]) * g_ref[0, :, sl]).astype(y_ref.dtype)

    @pl.when(c == n_chunks - 1)
    def _():
        for pi in range(n_pairs):
            sout_ref[0, 2 * pi] = wt_scr[pi, 0:HD, 0:HD]
            sout_ref[0, 2 * pi + 1] = wt_scr[pi, HD:2 * HD, HD:2 * HD]


def _wkv(r, k, v, ap, b, ld, g, s0, l, wts, *, B, T):
    RW = r.shape[-1]
    NH = RW // RWKV_HEAD
    WT = _pick(RW, (1024, 512, 256, 128))
    n_pairs = WT // LANES
    n_chunks = T // CHUNK
    x_spec = pl.BlockSpec((1, CHUNK, WT), lambda bb, j, c: (bb, c, j))
    s_spec = pl.BlockSpec((1, 2 * n_pairs, RWKV_HEAD, RWKV_HEAD), lambda bb, j, c: (bb, j, 0, 0))
    p_spec = pl.BlockSpec((None, 1, WT), lambda bb, j, c: (l, 0, j))
    sh = lambda t: t.reshape(B, T, RW)
    y, s_new = pl.pallas_call(
        functools.partial(_wkv_kernel, n_pairs=n_pairs, n_chunks=n_chunks),
        grid=(B, RW // WT, n_chunks),
        in_specs=[x_spec] * 7 + [s_spec, p_spec, p_spec, p_spec],
        out_specs=[x_spec, s_spec],
        out_shape=[jax.ShapeDtypeStruct((B, T, RW), BF16),
                   jax.ShapeDtypeStruct((B, NH, RWKV_HEAD, RWKV_HEAD), F32)],
        scratch_shapes=[pltpu.VMEM((n_pairs, LANES, LANES), F32)],
        compiler_params=_cparams(("parallel", "parallel", "arbitrary")),
        name="wkv_chunk",
    )(sh(r), sh(k), sh(v), sh(ap), sh(b), sh(ld), sh(g), s0,
      wts["bonus_rk"], wts["lnx_g"], wts["lnx_b"])
    return y.reshape(B * T, RW), s_new


def _layer(x, l, wb, wts, rope, *, B, T, caches, wkv0, shift0):
    D = x.shape[1]
    M = B * T
    W = wb["w_att_o"].shape[1]
    H = W // HEAD_DIM
    ZW = wts["mu_p"].shape[-1]
    cos, sin = rope
    xn = _rmsnorm(x, wts["norm1_g"], l)
    att_cols = 3 * N_GROUPS * W
    proj_a = _matmul(xn, wb["w_in"], l, n_cols=att_cols, out_dtype=F32, name="in_proj_att")
    proj_z = _matmul(xn, wb["w_in"], l, n_cols=ZW, w_col0=att_cols, out_dtype=F32, name="in_proj_rwkv")
    proj_g = _matmul(xn, wb["w_gate"], l, out_dtype=F32, name="in_proj_gate")
    if caches is None:
        att, knews, vnews = _band_attention(proj_a, cos, sin, wts["q_norm_g"], wts["k_norm_g"], l,
                                            B=B, S=T, H=H)
    else:
        att, kn = _cached_attention(proj_a[:M].reshape(B, 1, -1), caches, cos, sin,
                                    wts["q_norm_g"], wts["k_norm_g"], l, B=B, H=H)
        att = jnp.pad(att.reshape(B, W).astype(BF16), ((0, x.shape[0] - M), (0, 0)))
        kn = kn.reshape(B, 1, N_GROUPS, H, HEAD_DIM)
        knews = [kn[:, :, gi] for gi in range(N_GROUPS)]
        vv = proj_a[:M, 2 * N_GROUPS * W:].reshape(B, 1, N_GROUPS, H, HEAD_DIM)
        vnews = [vv[:, :, gi] for gi in range(N_GROUPS)]
    if caches is None:
        r, k, v, ap, b, ld, g = _rwkv_prep(proj_z, None, l, wts, rows_per_seq=T, single_step=False)
        yr, wkv_new = _wkv(r, k, v, ap, b, ld, g, wkv0, l, wts, B=B, T=T)
    else:
        zprev = jnp.pad(shift0, ((0, x.shape[0] - M), (0, ZW - shift0.shape[1])))
        outs = _rwkv_prep(proj_z, zprev, l, wts, rows_per_seq=1, single_step=True)
        padded = [jnp.pad(t[:M].reshape(B, 1, -1), ((0, 0), (0, CHUNK - 1), (0, 0))).reshape(B * CHUNK, -1)
                  for t in outs]
        yr, wkv_new = _wkv(*padded, wkv0, l, wts, B=B, T=CHUNK)
        yr = jnp.pad(yr.reshape(B, CHUNK, -1)[:, 0], ((0, x.shape[0] - M), (0, 0)))
    t1 = _matmul(att, wb["w_att_o"], l, out_dtype=F32, epilogue=_ep_gate,
                 extras=[(proj_g, 0)], name="att_out")
    merged = _matmul(yr, wb["w_rwkv_o"], l, out_dtype=BF16, epilogue=_ep_gate_add,
                     extras=[(proj_g, D), (t1, 0)], name="rwkv_out_merge")
    x1 = _matmul(merged, wb["w_out"], l, out_dtype=F32, epilogue=_ep_residual,
                 extras=[(x, 0)], tn=_pick(D, (512, 256, 128)), name="w_out")
    hn = _rmsnorm(x1, wts["norm2_g"], l)
    hh = _matmul(hn, wb["w_ffn_up"], l, out_dtype=BF16, epilogue=_ep_relu2, name="ffn_up")
    x2 = _matmul(hh, wb["w_ffn_down"], l, out_dtype=F32, epilogue=_ep_residual,
                 extras=[(x1, 0)], tk=_pick(hh.shape[1], (2048, 1024, 512, 256, 128)), name="ffn_down")
    return x2, knews, vnews, proj_z, wkv_new


def _rope_tables(pos):
    half = HEAD_DIM // 2
    inv_freq = ROPE_THETA ** (-jnp.arange(half, dtype=F32) / half)
    ang = pos.astype(F32)[:, None] * inv_freq[None, :]
    cos, sin = jnp.cos(ang), jnp.sin(ang)
    return jnp.concatenate([cos, cos], axis=-1), jnp.concatenate([-sin, sin], axis=-1)


def _split_hi_lo(w):
    hi = w.astype(BF16)
    lo = (w - hi.astype(F32)).astype(BF16)
    return jnp.stack([hi, lo], axis=1)


def kernel(x_prompt, x_sample, cache_kv_w128, cache_kv_w512, cache_kv_w2048, state_wkv, state_shift,
           norm1_g, w_in, q_norm_g, k_norm_g, shift_mu, decay_w0, decay_up, iclr_a0, iclr_up, gate_up,
           key_k, key_a, bonus_rk, lnx_g, lnx_b, w_att_o, w_rwkv_o, w_out, norm2_g, w_ffn_up, w_ffn_down):
    B, S, D = x_prompt.shape
    Bs, Ts, _ = x_sample.shape
    assert Ts == 1
    DEPTH = w_in.shape[0]
    RW = decay_w0.shape[-1]
    NH = RW // RWKV_HEAD
    W = w_att_o.shape[1]
    H = W // HEAD_DIM
    att_cols = 3 * N_GROUPS * W
    rwkv_cols = 3 * RW + LORA_W + LORA_A + LORA_G
    ZW = -(-(3 * RW + LORA_W + LORA_A + LORA_G_PAD) // 1024) * 1024
    Ms = 16

    vec = lambda t: t.reshape(DEPTH, 1, -1)
    wts = {
        "norm1_g": norm1_g, "norm2_g": norm2_g, "q_norm_g": q_norm_g, "k_norm_g": k_norm_g,
        "mu_p": vec(jnp.pad(shift_mu, ((0, 0), (0, ZW - rwkv_cols)))),
        "decay_w0": vec(decay_w0), "iclr_a0": vec(iclr_a0), "key_k": vec(key_k), "key_a": vec(key_a),
        "bonus_rk": vec(bonus_rk), "lnx_g": vec(lnx_g), "lnx_b": vec(lnx_b),
        "decay_up2": _split_hi_lo(decay_up), "iclr_up2": _split_hi_lo(iclr_up),
        "gate_up2": _split_hi_lo(jnp.pad(gate_up, ((0, 0), (0, LORA_G_PAD - LORA_G), (0, 0)))),
    }
    pos_p = jnp.arange(S, dtype=jnp.int32)
    rope_p = _rope_tables(pos_p)
    caches = (cache_kv_w128, cache_kv_w512, cache_kv_w2048)

    yp = x_prompt.reshape(B * S, D)
    ys = jnp.pad(x_sample.reshape(Bs, D), ((0, Ms - Bs), (0, 0)))
    zeros_wkv = jnp.zeros((B, NH, RWKV_HEAD, RWKV_HEAD), F32)
    names = ("pk0", "pk1", "pk2", "pv0", "pv1", "pv2", "pwkv", "pshift",
             "sk0", "sk1", "sk2", "sv0", "sv1", "sv2", "swkv", "sshift")
    outs = {k_: [] for k_ in names}
    rope_s = _rope_tables(PAST_LEN + jnp.arange(Ts, dtype=jnp.int32))
    assert att_cols + ZW <= w_in.shape[2]
    wb = {"w_in": w_in.astype(BF16), "w_gate": w_in[:, :, att_cols + rwkv_cols:].astype(BF16),
          "w_att_o": w_att_o.astype(BF16), "w_rwkv_o": w_rwkv_o.astype(BF16), "w_out": w_out.astype(BF16),
          "w_ffn_up": w_ffn_up.astype(BF16), "w_ffn_down": w_ffn_down.astype(BF16)}
    for l in range(DEPTH):
        yp, kn_p, vn_p, z_p, wkv_p = _layer(yp, l, wb, wts, rope_p, B=B, T=S, caches=None,
                                            wkv0=zeros_wkv, shift0=None)
        ys, kn_s, vn_s, z_s, wkv_s = _layer(ys, l, wb, wts, rope_s, B=Bs, T=1, caches=caches,
                                            wkv0=state_wkv[l], shift0=state_shift[l])
        for gi in range(N_GROUPS):
            outs[f"pk{gi}"].append(kn_p[gi])
            outs[f"pv{gi}"].append(vn_p[gi])
            outs[f"sk{gi}"].append(kn_s[gi])
            outs[f"sv{gi}"].append(vn_s[gi])
        outs["pwkv"].append(wkv_p)
        outs["swkv"].append(wkv_s)
        outs["pshift"].append(z_p.reshape(B, S, ZW)[:, -1, :rwkv_cols])
        outs["sshift"].append(z_s[:Bs, :rwkv_cols])
    st = lambda n: jnp.stack(outs[n])
    new_kv_p = [jnp.stack([st(f"pk{gi}"), st(f"pv{gi}")], axis=3) for gi in range(N_GROUPS)]
    new_kv_s = [_shift_append(caches[gi], jnp.stack([st(f"sk{gi}"), st(f"sv{gi}")], axis=3))
                for gi in range(N_GROUPS)]
    return (yp.reshape(B, S, D), ys[:Bs].reshape(Bs, 1, D), *new_kv_p, st("pwkv"), st("pshift"),
            *new_kv_s, st("swkv"), st("sshift"))
```

```python
import functools
import math

import jax
import jax.numpy as jnp
from jax import lax
from jax.experimental import pallas as pl
from jax.experimental.pallas import tpu as pltpu

F32 = jnp.float32
BF16 = jnp.bfloat16

N_GROUPS = 3
ATT_GROUPS = ((128, 1), (512, 4), (2048, 16))
HEAD_DIM = 128
RWKV_HEAD = 64
LORA_W = 128
LORA_A = 128
LORA_G = 480
ROPE_THETA = 10000.0
PAST_LEN = 8192
NEG_INF = -1e30
GN_EPS = 64e-5
NORM_EPS = 1e-6

LANES = 128
VMEM_LIMIT_BYTES = 56 * 1024 * 1024

CHUNK = 64
LORA_G_PAD = 512
SPAN = 128


def _cparams(sem):
    return pltpu.CompilerParams(dimension_semantics=sem, vmem_limit_bytes=VMEM_LIMIT_BYTES)


def _pick(n, prefs):
    for p in prefs:
        if n % p == 0:
            return p
    return n


def _split(x, n):
    parts, rem = [], x
    for i in range(n):
        p = rem.astype(BF16)
        parts.append(p)
        if i + 1 < n:
            rem = rem - p.astype(F32)
    return parts


_NN = (((1,), (0,)), ((), ()))
_NT = (((1,), (1,)), ((), ()))
_TN = (((0,), (0,)), ((), ()))


def _mmp(a_parts, b_parts, dims=_NN):
    order = max(len(a_parts), len(b_parts))
    acc = None
    for i, a in enumerate(a_parts):
        for j, b in enumerate(b_parts):
            if i + j < order:
                t = lax.dot_general(a, b, dims, preferred_element_type=F32)
                acc = t if acc is None else acc + t
    return acc


def _rms_kernel(x_ref, g_ref, o_ref):
    x = x_ref[...]
    ms = jnp.mean(x * x, axis=-1, keepdims=True)
    o_ref[...] = (x * lax.rsqrt(ms + NORM_EPS) * g_ref[...]).astype(o_ref.dtype)


def _rmsnorm(x, g, l):
    M, D = x.shape
    tm = _pick(M, (256, 128, 64, 32, 16, 8))
    return pl.pallas_call(
        _rms_kernel,
        grid=(M // tm,),
        in_specs=[pl.BlockSpec((tm, D), lambda i: (i, 0)),
                  pl.BlockSpec((None, 1, D), lambda i: (l, 0, 0))],
        out_specs=pl.BlockSpec((tm, D), lambda i: (i, 0)),
        out_shape=jax.ShapeDtypeStruct((M, D), BF16),
        compiler_params=_cparams(("parallel",)),
        name="rmsnorm",
    )(x, g.reshape(g.shape[0], 1, D))


def _ep_none(acc):
    return acc


def _ep_relu2(acc):
    return jnp.square(jnp.maximum(acc, 0.0))


def _ep_gate(acc, g):
    return jax.nn.sigmoid(g) * acc


def _ep_gate_add(acc, g, t):
    return t + jax.nn.sigmoid(g) * acc


def _ep_residual(acc, x):
    return x + acc


def _mm_kernel(*refs, nk, n_extra, epilogue):
    a_ref, w_ref = refs[0], refs[1]
    extra = refs[2:2 + n_extra]
    o_ref = refs[2 + n_extra]
    if nk == 1:
        acc = jnp.dot(a_ref[...], w_ref[...], preferred_element_type=F32)
        o_ref[...] = epilogue(acc, *[e[...] for e in extra]).astype(o_ref.dtype)
        return
    acc_ref = refs[3 + n_extra]
    k = pl.program_id(2)

    @pl.when(k == 0)
    def _():
        acc_ref[...] = jnp.zeros_like(acc_ref)

    acc_ref[...] += jnp.dot(a_ref[...], w_ref[...], preferred_element_type=F32)

    @pl.when(k == nk - 1)
    def _():
        o_ref[...] = epilogue(acc_ref[...], *[e[...] for e in extra]).astype(o_ref.dtype)


def _matmul(a, w, l, *, out_dtype, n_cols=None, w_col0=0, epilogue=_ep_none, extras=(),
            tm=None, tn=None, tk=None, name="matmul"):
    M, K = a.shape
    N = n_cols or w.shape[2]
    tm = tm or _pick(M, (1024, 512, 256, 128, 64, 32, 16))
    tn = tn or _pick(math.gcd(N, w_col0) if w_col0 else N, (1024, 512, 256, 128))
    tk = tk or _pick(K, (4096, 2048, 1024, 512, 256, 128))
    nk = K // tk
    assert w_col0 % tn == 0 and N % tn == 0 and M % tm == 0
    in_specs = [pl.BlockSpec((tm, tk), lambda i, j, k: (i, k)),
                pl.BlockSpec((None, tk, tn), lambda i, j, k, _o=w_col0 // tn: (l, k, j + _o))]
    args = [a, w]
    for arr, off in extras:
        assert off % tn == 0
        in_specs.append(pl.BlockSpec((tm, tn), lambda i, j, k, _o=off // tn: (i, j + _o)))
        args.append(arr)
    scratch = [] if nk == 1 else [pltpu.VMEM((tm, tn), F32)]
    return pl.pallas_call(
        functools.partial(_mm_kernel, nk=nk, n_extra=len(extras), epilogue=epilogue),
        grid=(M // tm, N // tn, nk),
        in_specs=in_specs,
        out_specs=pl.BlockSpec((tm, tn), lambda i, j, k: (i, j)),
        out_shape=jax.ShapeDtypeStruct((M, N), out_dtype),
        scratch_shapes=scratch,
        compiler_params=_cparams(("parallel", "parallel", "arbitrary")),
        name=name,
    )(*args)


def _norm_rope(x, g, cos, sin):
    ms = jnp.mean(x * x, axis=-1, keepdims=True)
    y = x * lax.rsqrt(ms + NORM_EPS) * g
    return y * cos + pltpu.roll(y, HEAD_DIM // 2, 1) * sin


ROW_CHUNK = 256


def _band_attn_kernel(q0, k0, v0, q1, k1, v1, q2, k2, v2, cos_ref, sin_ref, qg_ref, kg_ref,
                      att_ref, kn0, kn1, kn2, vn0, vn1, vn2, qs, ks, os_, ls, *, S):
    scale = HEAD_DIM ** -0.5
    qkv = ((q0, k0, v0, kn0, vn0), (q1, k1, v1, kn1, vn1), (q2, k2, v2, kn2, vn2))
    for gi, (q_ref, k_ref, v_ref, kn_ref, vn_ref) in enumerate(qkv):
        win, dil = ATT_GROUPS[gi]
        L = S // dil
        for c in range(S // ROW_CHUNK):
            sl = slice(c * ROW_CHUNK, (c + 1) * ROW_CHUNK)
            cos, sin = cos_ref[sl, :], sin_ref[sl, :]
            qs[sl, :] = _norm_rope(q_ref[0, sl, :], qg_ref[...], cos, sin)
            ks[sl, :] = _norm_rope(k_ref[0, sl, :], kg_ref[...], cos, sin)
        kn_ref[0] = ks[S - win:, :]
        vn_ref[0] = v_ref[0, S - win:, :]

        def rows(r, start, n, dil=dil):
            return pl.ds(start, n) if dil == 1 else pl.ds(r + start * dil, n, stride=dil)

        for r in range(dil):
            for i in range(L // SPAN):
                lo = max(i - 1, 0) * SPAN
                nkeys = (i + 1) * SPAN - lo
                qb = qs[rows(r, i * SPAN, SPAN), :].astype(BF16)
                kb = ks[rows(r, lo, nkeys), :].astype(BF16)
                vb = v_ref[0, rows(r, lo, nkeys), :].astype(BF16)
                s = lax.dot_general(qb, kb, _NT, preferred_element_type=F32) * scale
                qi = lax.broadcasted_iota(jnp.int32, (SPAN, nkeys), 0) + (i * SPAN - lo)
                ki = lax.broadcasted_iota(jnp.int32, (SPAN, nkeys), 1)
                s = jnp.where((ki <= qi) & (ki >= qi - SPAN), s, NEG_INF)
                m = jnp.max(s, axis=-1, keepdims=True)
                p = jnp.exp(s - m)
                den = jnp.sum(p, axis=-1, keepdims=True)
                o = jnp.dot(p.astype(BF16), vb, preferred_element_type=F32)
                os_[gi, rows(r, i * SPAN, SPAN), :] = o / den
                ls[gi, rows(r, i * SPAN, SPAN), :] = jnp.broadcast_to(m + jnp.log(den), (SPAN, HEAD_DIM))
    for c in range(S // ROW_CHUNK):
        sl = slice(c * ROW_CHUNK, (c + 1) * ROW_CHUNK)
        a0, a1, a2 = ls[0, sl, :], ls[1, sl, :], ls[2, sl, :]
        mx = jnp.maximum(jnp.maximum(a0, a1), a2)
        e0, e1, e2 = jnp.exp(a0 - mx), jnp.exp(a1 - mx), jnp.exp(a2 - mx)
        num = e0 * os_[0, sl, :] + e1 * os_[1, sl, :] + e2 * os_[2, sl, :]
        att_ref[0, sl, :] = (num / (e0 + e1 + e2)).astype(att_ref.dtype)


def _band_attention(proj, cos, sin, qg, kg, l, *, B, S, H):
    NP = proj.shape[1]
    G = N_GROUPS
    W = H * HEAD_DIM
    for win, dil in ATT_GROUPS:
        assert win // dil == SPAN and (S // dil) % SPAN == 0 and S % ROW_CHUNK == 0 and win <= S
    pv = proj.reshape(B, S, NP)
    qkv_specs = [pl.BlockSpec((1, S, HEAD_DIM), lambda b, h, _o=(c * G + gi) * H: (b, 0, _o + h))
                 for gi in range(G) for c in range(3)]
    tab_spec = pl.BlockSpec((S, HEAD_DIM), lambda b, h: (0, 0))
    g_spec = pl.BlockSpec((None, 1, HEAD_DIM), lambda b, h: (l, 0, 0))
    win_specs = [pl.BlockSpec((1, win, HEAD_DIM), lambda b, h: (b, 0, h)) for win, _ in ATT_GROUPS]
    win_shapes = [jax.ShapeDtypeStruct((B, win, W), F32) for win, _ in ATT_GROUPS]
    outs = pl.pallas_call(
        functools.partial(_band_attn_kernel, S=S),
        grid=(B, H),
        in_specs=qkv_specs + [tab_spec, tab_spec, g_spec, g_spec],
        out_specs=[pl.BlockSpec((1, S, HEAD_DIM), lambda b, h: (b, 0, h))] + win_specs + win_specs,
        out_shape=[jax.ShapeDtypeStruct((B, S, W), BF16)] + win_shapes + win_shapes,
        scratch_shapes=[pltpu.VMEM((S, HEAD_DIM), F32), pltpu.VMEM((S, HEAD_DIM), F32),
                        pltpu.VMEM((G, S, HEAD_DIM), F32), pltpu.VMEM((G, S, HEAD_DIM), F32)],
        compiler_params=_cparams(("parallel", "parallel")),
        name="band_attn",
    )(*([pv] * (3 * G)), cos, sin, qg.reshape(-1, 1, HEAD_DIM), kg.reshape(-1, 1, HEAD_DIM))
    att = outs[0].reshape(B * S, W)
    heads = lambda t: t.reshape(B, t.shape[1], H, HEAD_DIM)
    return att, [heads(t) for t in outs[1:1 + G]], [heads(t) for t in outs[1 + G:]]


def _cached_attn_kernel(p_ref, c0_ref, c1_ref, c2_ref, cos_ref, sin_ref, qg_ref, kg_ref,
                        att_ref, knew_ref, *, H):
    cos = cos_ref[...]
    sin = sin_ref[...]
    scale = HEAD_DIM ** -0.5
    G = N_GROUPS
    W = H * HEAD_DIM
    caches = (c0_ref, c1_ref, c2_ref)
    for h in range(H):
        ms, dens, outs = [], [], []
        for gi in range(G):
            def col(c):
                o = ((c * G + gi) * H + h) * HEAD_DIM
                return p_ref[0, :, o:o + HEAD_DIM]
            q = _norm_rope(col(0), qg_ref[...], cos, sin)
            k = _norm_rope(col(1), kg_ref[...], cos, sin)
            v = col(2)
            knew_ref[0, :, (gi * H + h) * HEAD_DIM:(gi * H + h + 1) * HEAD_DIM] = k
            kc = caches[gi][0, 0, :, 0, 0, h, :]
            vc = caches[gi][0, 0, :, 0, 1, h, :]
            q8 = jnp.broadcast_to(q, (8, HEAD_DIM)).astype(BF16)
            s_c = lax.dot_general(q8, kc.astype(BF16), _NT, preferred_element_type=F32)[0:1] * scale
            s_n = jnp.sum(q.astype(BF16).astype(F32) * k.astype(BF16).astype(F32),
                          axis=-1, keepdims=True) * scale
            m = jnp.maximum(jnp.max(s_c, axis=-1, keepdims=True), s_n)
            p_c = jnp.exp(s_c - m)
            p_n = jnp.exp(s_n - m)
            den = jnp.sum(p_c, axis=-1, keepdims=True) + p_n
            p8 = jnp.broadcast_to(p_c, (8, SPAN)).astype(BF16)
            o = jnp.dot(p8, vc.astype(BF16), preferred_element_type=F32)[0:1]
            o = o + p_n.astype(BF16).astype(F32) * v.astype(BF16).astype(F32)
            ms.append(m)
            dens.append(den)
            outs.append(o)
        mx = jnp.maximum(jnp.maximum(ms[0], ms[1]), ms[2])
        num = jnp.zeros((1, HEAD_DIM), F32)
        tot = jnp.zeros((1, 1), F32)
        for gi in range(G):
            e = jnp.exp(ms[gi] - mx)
            num = num + e * outs[gi]
            tot = tot + e * dens[gi]
        att_ref[0, :, h * HEAD_DIM:(h + 1) * HEAD_DIM] = (num / tot).astype(att_ref.dtype)


def _cached_attention(proj_s, caches, cos, sin, qg, kg, l, *, B, H):
    NP = proj_s.shape[-1]
    W = H * HEAD_DIM
    views, specs = [], []
    for gi, (win, dil) in enumerate(ATT_GROUPS):
        c = caches[gi]
        Lb = c.shape[2]
        assert Lb == win and Lb // dil == SPAN
        views.append(c.reshape(c.shape[0], B, SPAN, dil, 2, H, HEAD_DIM))
        specs.append(pl.BlockSpec((1, 1, SPAN, 1, 2, H, HEAD_DIM), lambda b: (l, b, 0, 0, 0, 0, 0)))
    tab = pl.BlockSpec((1, HEAD_DIM), lambda b: (0, 0))
    gsp = pl.BlockSpec((None, 1, HEAD_DIM), lambda b: (l, 0, 0))
    return pl.pallas_call(
        functools.partial(_cached_attn_kernel, H=H),
        grid=(B,),
        in_specs=[pl.BlockSpec((1, 1, NP), lambda b: (b, 0, 0))] + specs + [tab, tab, gsp, gsp],
        out_specs=[pl.BlockSpec((1, 1, W), lambda b: (b, 0, 0)),
                   pl.BlockSpec((1, 1, N_GROUPS * W), lambda b: (b, 0, 0))],
        out_shape=[jax.ShapeDtypeStruct((B, 1, W), F32),
                   jax.ShapeDtypeStruct((B, 1, N_GROUPS * W), F32)],
        compiler_params=_cparams(("parallel",)),
        name="cached_attn",
    )(proj_s, *views, cos, sin, qg.reshape(-1, 1, HEAD_DIM), kg.reshape(-1, 1, HEAD_DIM))


def _shift_append_kernel(c_ref, nxt_ref, new_ref, o_ref, *, R, n_tiles):
    j = pl.program_id(2)
    o_ref[0, 0, 0:R - 1] = c_ref[0, 0, 1:R]
    o_ref[0, 0, R - 1] = jnp.where(j == n_tiles - 1, new_ref[0, 0, 0], nxt_ref[0, 0, 0])


def _shift_append(cache, new_rows):
    DEPTH, B, Lb = cache.shape[:3]
    tail = cache.shape[3:]
    R = _pick(Lb, (256, 128, 64, 32, 16, 8))
    n_tiles = Lb // R
    z = (0,) * len(tail)
    return pl.pallas_call(
        functools.partial(_shift_append_kernel, R=R, n_tiles=n_tiles),
        grid=(DEPTH, B, n_tiles),
        in_specs=[pl.BlockSpec((1, 1, R) + tail, lambda l, b, j: (l, b, j) + z),
                  pl.BlockSpec((1, 1, 1) + tail, lambda l, b, j: (l, b, jnp.minimum((j + 1) * R, Lb - 1)) + z),
                  pl.BlockSpec((1, 1, 1) + tail, lambda l, b, j: (l, b, 0) + z)],
        out_specs=pl.BlockSpec((1, 1, R) + tail, lambda l, b, j: (l, b, j) + z),
        out_shape=jax.ShapeDtypeStruct(cache.shape, cache.dtype),
        compiler_params=_cparams(("parallel", "parallel", "parallel")),
        name="shift_append",
    )(cache, cache, new_rows)


def _head_sum(x, jmat):
    return _mmp(_split(x, 2), [jmat])


def _rwkv_prep_kernel(z_ref, zp_ref, mu_ref, w0_ref, a0_ref, kk_ref, ka_ref,
                      du_ref, iu_ref, gu_ref,
                      r_ref, k_ref, v_ref, ap_ref, b_ref, ld_ref, g_ref,
                      *, RW, single_step, tiles_per_seq):
    z = z_ref[...]
    if single_step:
        zp = zp_ref[...]
    else:
        first = pl.program_id(0) % tiles_per_seq == 0
        prev_row = jnp.where(first, 0.0, zp_ref[7:8, :])
        rows = lax.broadcasted_iota(jnp.int32, z.shape, 0)
        zp = jnp.where(rows == 0, prev_row, pltpu.roll(z, 1, 0))
    zs = z + mu_ref[...] * (zp - z)
    r = zs[:, 0:RW]
    k = zs[:, RW:2 * RW]
    v = zs[:, 2 * RW:3 * RW]
    o = 3 * RW
    w_lo = zs[:, o:o + LORA_W]
    a_lo = zs[:, o + LORA_W:o + LORA_W + LORA_A]
    g_lo = zs[:, o + LORA_W + LORA_A:o + LORA_W + LORA_A + LORA_G_PAD]

    def lora(x, u_ref):
        return _mmp(_split(x, 2), [u_ref[0], u_ref[1]])

    xw = w0_ref[...] + lora(jnp.tanh(w_lo), du_ref)
    ld_ref[...] = -math.exp(-0.5) * jax.nn.sigmoid(xw)
    a = jax.nn.sigmoid(a0_ref[...] + lora(a_lo, iu_ref))
    g_ref[...] = lora(jax.nn.sigmoid(g_lo), gu_ref)
    r_ref[...] = r
    v_ref[...] = v
    k_ref[...] = k * (1.0 + (a - 1.0) * ka_ref[...])
    ri = lax.broadcasted_iota(jnp.int32, (LANES, LANES), 0) // RWKV_HEAD
    ci = lax.broadcasted_iota(jnp.int32, (LANES, LANES), 1) // RWKV_HEAD
    jmat = jnp.where(ri == ci, 1.0, 0.0).astype(BF16)
    for t in range(RW // LANES):
        sl = slice(t * LANES, (t + 1) * LANES)
        kk = k[:, sl] * kk_ref[:, sl]
        ss = _head_sum(kk * kk, jmat)
        kk = kk * lax.rsqrt(jnp.maximum(ss, 1e-24))
        ap_ref[:, sl] = -kk
        b_ref[:, sl] = kk * a[:, sl]


def _rwkv_prep(proj, zprev, l, wts, *, rows_per_seq, single_step):
    M = proj.shape[0]
    RW = wts["decay_w0"].shape[-1]
    ZW = wts["mu_p"].shape[-1]
    tm = 8 if single_step else _pick(rows_per_seq, (128, 64, 32, 16, 8))
    z_spec = pl.BlockSpec((tm, ZW), lambda i: (i, 0))
    if single_step:
        zp_arr, zp_spec = zprev, pl.BlockSpec((tm, ZW), lambda i: (i, 0))
    else:
        zp_arr = proj
        zp_spec = pl.BlockSpec((8, ZW), lambda i: (jnp.maximum(i * (tm // 8) - 1, 0), 0))

    def vec(n):
        return pl.BlockSpec((None, 1, n), lambda i: (l, 0, 0))

    def up(kdim):
        return pl.BlockSpec((None, 2, kdim, RW), lambda i: (l, 0, 0, 0))

    o_spec = pl.BlockSpec((tm, RW), lambda i: (i, 0))
    outs = pl.pallas_call(
        functools.partial(_rwkv_prep_kernel, RW=RW, single_step=single_step,
                          tiles_per_seq=max(rows_per_seq // tm, 1)),
        grid=(M // tm,),
        in_specs=[z_spec, zp_spec, vec(ZW), vec(RW), vec(RW), vec(RW), vec(RW),
                  up(LORA_W), up(LORA_A), up(LORA_G_PAD)],
        out_specs=[o_spec] * 7,
        out_shape=[jax.ShapeDtypeStruct((M, RW), F32)] * 7,
        compiler_params=_cparams(("parallel",)),
        name="rwkv_prep",
    )(proj, zp_arr, wts["mu_p"], wts["decay_w0"], wts["iclr_a0"], wts["key_k"], wts["key_a"],
      wts["decay_up2"], wts["iclr_up2"], wts["gate_up2"])
    return outs


WKV_PARTS = 1
INV_BASE = 4


def _mm(a, b, dims=_NN, parts=WKV_PARTS):
    return _mmp(_split(a, parts), _split(b, parts), dims)


def _each(f, *lists):
    return [f(*xs) for xs in zip(*lists)]


def _wkv_pairs(r, k, v, ap, b, ld, wt, consts):
    C = CHUNK
    C2 = 2 * C
    tri, m2, strict, incl, eye = consts
    cat0 = lambda *xs: jnp.concatenate(xs, axis=0)
    lc = _each(lambda x: _mmp([tri], _split(x, 3)), ld)
    lt = [x[C - 1:C, :] for x in lc]

    def stack(x):
        return jnp.where(m2, jnp.concatenate([x, x], axis=0), 0.0)

    ar2 = _each(lambda ap_, r_, lc_, ld_: cat0(stack(ap_ * jnp.exp(lc_ - ld_)), stack(r_ * jnp.exp(lc_))),
                ap, r, lc, ld)
    bk2 = _each(lambda b_, k_, lc_: cat0(stack(b_ * jnp.exp(-lc_)), stack(k_ * jnp.exp(-lc_))), b, k, lc)
    bke = _each(lambda b_, k_, lc_, lt_: cat0(stack(b_ * jnp.exp(lt_ - lc_)), stack(k_ * jnp.exp(lt_ - lc_))),
                b, k, lc, lt)
    v2 = _each(stack, v)
    a4 = _each(lambda x_, y_: _mm(x_, y_, _NT), ar2, bk2)
    n_ab = [jnp.where(strict, a[0:C2, 0:C2], 0.0) for a in a4]
    a_akrk = [cat0(jnp.where(strict, a[0:C2, C2:], 0.0), jnp.where(incl, a[C2:, C2:], 0.0)) for a in a4]
    a_rb = [jnp.where(incl, a[C2:, 0:C2], 0.0) for a in a4]
    sw = _each(lambda x_, w_: _mm(x_, w_, _NT), ar2, wt)
    av = _each(_mm, a_akrk, v2)
    bi = lax.broadcasted_iota(jnp.int32, (C2, C2), 0)
    bj = lax.broadcasted_iota(jnp.int32, (C2, C2), 1)
    n0 = [jnp.where((bi // INV_BASE) == (bj // INV_BASE), n, 0.0) for n in n_ab]
    x = _each(lambda n_, q_: (eye + n_) + _mm(q_, eye + n_), n0, _each(_mm, n0, n0))
    blk = INV_BASE
    while blk < C:
        off_mask = ((bi // (2 * blk)) == (bj // (2 * blk))) & ((bi // blk) != (bj // blk))
        t_off = _each(lambda t_, n_: _mm(t_, jnp.where(off_mask, n_, 0.0)), x, n_ab)
        x = _each(lambda t_, to_: t_ + _mm(to_, t_), x, t_off)
        blk *= 2
    u2 = _each(lambda x_, s_, a_: _mm(x_, s_[0:C2] + a_[0:C2]), x, sw, av)
    y2 = _each(lambda s_, a_, rb_, u_: s_[C2:] + a_[C2:] + _mm(rb_, u_), sw, av, a_rb, u2)
    y = [t[0:C] + t[C:C2] for t in y2]
    wt_new = _each(lambda w_, lt_, u_, v_, e_: w_ * jnp.exp(lt_) + _mm(cat0(u_, v_), e_, _TN),
                   wt, lt, u2, v2, bke)
    return y, wt_new


def _wkv_kernel(r_ref, k_ref, v_ref, ap_ref, b_ref, ld_ref, g_ref, s0_ref,
                brk_ref, lg_ref, lb_ref, y_ref, sout_ref, wt_scr, *, n_pairs, n_chunks):
    C = CHUNK
    c = pl.program_id(2)
    HD = RWKV_HEAD

    @pl.when(c == 0)
    def _():
        wt_scr[...] = jnp.zeros_like(wt_scr)
        for pi in range(n_pairs):
            wt_scr[pi, 0:HD, 0:HD] = s0_ref[0, 2 * pi]
            wt_scr[pi, HD:2 * HD, HD:2 * HD] = s0_ref[0, 2 * pi + 1]

    ri = lax.broadcasted_iota(jnp.int32, (2 * C, LANES), 0)
    ci = lax.broadcasted_iota(jnp.int32, (2 * C, LANES), 1)
    tri_r = lax.broadcasted_iota(jnp.int32, (C, C), 0)
    tri_c = lax.broadcasted_iota(jnp.int32, (C, C), 1)
    tri = jnp.where(tri_c <= tri_r, 1.0, 0.0).astype(BF16)
    m2 = (ri // C) == (ci // HD)
    same = (ri // C) == (ci // C)
    strict = same & ((ci % C) < (ri % C))
    incl = same & ((ci % C) <= (ri % C))
    eye = jnp.where(ri == ci, 1.0, 0.0)
    jmat = jnp.where((ri // HD) == (ci // HD), 1.0, 0.0).astype(BF16)
    consts = (tri, m2, strict, incl, eye)
    inv = 1.0 / HD
    sls = [slice(pi * LANES, (pi + 1) * LANES) for pi in range(n_pairs)]
    col = lambda ref: [ref[0, :, sl] for sl in sls]
    r, k, v = col(r_ref), col(k_ref), col(v_ref)
    y, wt_new = _wkv_pairs(r, k, v, col(ap_ref), col(b_ref), col(ld_ref),
                           [wt_scr[pi] for pi in range(n_pairs)], consts)
    for pi in range(n_pairs):
        wt_scr[pi] = wt_new[pi]
    mu = [_head_sum(t, jmat) * inv for t in y]
    d = _each(lambda y_, m_: y_ - m_, y, mu)
    var = [_head_sum(t * t, jmat) * inv for t in d]
    rk = _each(lambda r_, k_, sl: _head_sum(r_ * k_ * brk_ref[:, sl], jmat), r, k, sls)
    for pi, sl in enumerate(sls):
        yn = d[pi] * lax.rsqrt(var[pi] + GN_EPS) * lg_ref[:, sl] + lb_ref[:, sl]
        y_ref[0, :, sl] = ((yn + rk[pi] * v[pi]) * g_ref[0, :, sl]).astype(y_ref.dtype)

    @pl.when(c == n_chunks - 1)
    def _():
        for pi in range(n_pairs):
            sout_ref[0, 2 * pi] = wt_scr[pi, 0:HD, 0:HD]
            sout_ref[0, 2 * pi + 1] = wt_scr[pi, HD:2 * HD, HD:2 * HD]


def _wkv(r, k, v, ap, b, ld, g, s0, l, wts, *, B, T):
    RW = r.shape[-1]
    NH = RW // RWKV_HEAD
    WT = _pick(RW, (1024, 512, 256, 128))
    n_pairs = WT // LANES
    n_chunks = T // CHUNK
    x_spec = pl.BlockSpec((1, CHUNK, WT), lambda bb, j, c: (bb, c, j))
    s_spec = pl.BlockSpec((1, 2 * n_pairs, RWKV_HEAD, RWKV_HEAD), lambda bb, j, c: (bb, j, 0, 0))
    p_spec = pl.BlockSpec((None, 1, WT), lambda bb, j, c: (l, 0, j))
    sh = lambda t: t.reshape(B, T, RW)
    y, s_new = pl.pallas_call(
        functools.partial(_wkv_kernel, n_pairs=n_pairs, n_chunks=n_chunks),
        grid=(B, RW // WT, n_chunks),
        in_specs=[x_spec] * 7 + [s_spec, p_spec, p_spec, p_spec],
        out_specs=[x_spec, s_spec],
        out_shape=[jax.ShapeDtypeStruct((B, T, RW), BF16),
                   jax.ShapeDtypeStruct((B, NH, RWKV_HEAD, RWKV_HEAD), F32)],
        scratch_shapes=[pltpu.VMEM((n_pairs, LANES, LANES), F32)],
        compiler_params=_cparams(("parallel", "parallel", "arbitrary")),
        name="wkv_chunk",
    )(sh(r), sh(k), sh(v), sh(ap), sh(b), sh(ld), sh(g), s0,
      wts["bonus_rk"], wts["lnx_g"], wts["lnx_b"])
    return y.reshape(B * T, RW), s_new


def _layer(x, l, wb, wts, rope, *, B, T, caches, wkv0, shift0):
    D = x.shape[1]
    M = B * T
    W = wb["w_att_o"].shape[1]
    H = W // HEAD_DIM
    ZW = wts["mu_p"].shape[-1]
    cos, sin = rope
    xn = _rmsnorm(x, wts["norm1_g"], l)
    att_cols = 3 * N_GROUPS * W
    proj_a = _matmul(xn, wb["w_in"], l, n_cols=att_cols, out_dtype=F32, name="in_proj_att")
    proj_z = _matmul(xn, wb["w_in"], l, n_cols=ZW, w_col0=att_cols, out_dtype=F32, name="in_proj_rwkv")
    proj_g = _matmul(xn, wb["w_gate"], l, out_dtype=F32, name="in_proj_gate")
    if caches is None:
        att, knews, vnews = _band_attention(proj_a, cos, sin, wts["q_norm_g"], wts["k_norm_g"], l,
                                            B=B, S=T, H=H)
    else:
        att, kn = _cached_attention(proj_a[:M].reshape(B, 1, -1), caches, cos, sin,
                                    wts["q_norm_g"], wts["k_norm_g"], l, B=B, H=H)
        att = jnp.pad(att.reshape(B, W).astype(BF16), ((0, x.shape[0] - M), (0, 0)))
        kn = kn.reshape(B, 1, N_GROUPS, H, HEAD_DIM)
        knews = [kn[:, :, gi] for gi in range(N_GROUPS)]
        vv = proj_a[:M, 2 * N_GROUPS * W:].reshape(B, 1, N_GROUPS, H, HEAD_DIM)
        vnews = [vv[:, :, gi] for gi in range(N_GROUPS)]
    if caches is None:
        r, k, v, ap, b, ld, g = _rwkv_prep(proj_z, None, l, wts, rows_per_seq=T, single_step=False)
        yr, wkv_new = _wkv(r, k, v, ap, b, ld, g, wkv0, l, wts, B=B, T=T)
    else:
        zprev = jnp.pad(shift0, ((0, x.shape[0] - M), (0, ZW - shift0.shape[1])))
        outs = _rwkv_prep(proj_z, zprev, l, wts, rows_per_seq=1, single_step=True)
        padded = [jnp.pad(t[:M].reshape(B, 1, -1), ((0, 0), (0, CHUNK - 1), (0, 0))).reshape(B * CHUNK, -1)
                  for t in outs]
        yr, wkv_new = _wkv(*padded, wkv0, l, wts, B=B, T=CHUNK)
        yr = jnp.pad(yr.reshape(B, CHUNK, -1)[:, 0], ((0, x.shape[0] - M), (0, 0)))
    t1 = _matmul(att, wb["w_att_o"], l, out_dtype=F32, epilogue=_ep_gate,
                 extras=[(proj_g, 0)], name="att_out")
    merged = _matmul(yr, wb["w_rwkv_o"], l, out_dtype=BF16, epilogue=_ep_gate_add,
                     extras=[(proj_g, D), (t1, 0)], name="rwkv_out_merge")
    x1 = _matmul(merged, wb["w_out"], l, out_dtype=F32, epilogue=_ep_residual,
                 extras=[(x, 0)], tn=_pick(D, (512, 256, 128)), name="w_out")
    hn = _rmsnorm(x1, wts["norm2_g"], l)
    hh = _matmul(hn, wb["w_ffn_up"], l, out_dtype=BF16, epilogue=_ep_relu2, name="ffn_up")
    x2 = _matmul(hh, wb["w_ffn_down"], l, out_dtype=F32, epilogue=_ep_residual,
                 extras=[(x1, 0)], tk=_pick(hh.shape[1], (2048, 1024, 512, 256, 128)), name="ffn_down")
    return x2, knews, vnews, proj_z, wkv_new


def _rope_tables(pos):
    half = HEAD_DIM // 2
    inv_freq = ROPE_THETA ** (-jnp.arange(half, dtype=F32) / half)
    ang = pos.astype(F32)[:, None] * inv_freq[None, :]
    cos, sin = jnp.cos(ang), jnp.sin(ang)
    return jnp.concatenate([cos, cos], axis=-1), jnp.concatenate([-sin, sin], axis=-1)


def _split_hi_lo(w):
    hi = w.astype(BF16)
    lo = (w - hi.astype(F32)).astype(BF16)
    return jnp.stack([hi, lo], axis=1)


def kernel(x_prompt, x_sample, cache_kv_w128, cache_kv_w512, cache_kv_w2048, state_wkv, state_shift,
           norm1_g, w_in, q_norm_g, k_norm_g, shift_mu, decay_w0, decay_up, iclr_a0, iclr_up, gate_up,
           key_k, key_a, bonus_rk, lnx_g, lnx_b, w_att_o, w_rwkv_o, w_out, norm2_g, w_ffn_up, w_ffn_down):
    B, S, D = x_prompt.shape
    Bs, Ts, _ = x_sample.shape
    assert Ts == 1
    DEPTH = w_in.shape[0]
    RW = decay_w0.shape[-1]
    NH = RW // RWKV_HEAD
    W = w_att_o.shape[1]
    H = W // HEAD_DIM
    att_cols = 3 * N_GROUPS * W
    rwkv_cols = 3 * RW + LORA_W + LORA_A + LORA_G
    ZW = -(-(3 * RW + LORA_W + LORA_A + LORA_G_PAD) // 1024) * 1024
    Ms = 16

    vec = lambda t: t.reshape(DEPTH, 1, -1)
    wts = {
        "norm1_g": norm1_g, "norm2_g": norm2_g, "q_norm_g": q_norm_g, "k_norm_g": k_norm_g,
        "mu_p": vec(jnp.pad(shift_mu, ((0, 0), (0, ZW - rwkv_cols)))),
        "decay_w0": vec(decay_w0), "iclr_a0": vec(iclr_a0), "key_k": vec(key_k), "key_a": vec(key_a),
        "bonus_rk": vec(bonus_rk), "lnx_g": vec(lnx_g), "lnx_b": vec(lnx_b),
        "decay_up2": _split_hi_lo(decay_up), "iclr_up2": _split_hi_lo(iclr_up),
        "gate_up2": _split_hi_lo(jnp.pad(gate_up, ((0, 0), (0, LORA_G_PAD - LORA_G), (0, 0)))),
    }
    pos_p = jnp.arange(S, dtype=jnp.int32)
    rope_p = _rope_tables(pos_p)
    caches = (cache_kv_w128, cache_kv_w512, cache_kv_w2048)

    yp = x_prompt.reshape(B * S, D)
    ys = jnp.pad(x_sample.reshape(Bs, D), ((0, Ms - Bs), (0, 0)))
    zeros_wkv = jnp.zeros((B, NH, RWKV_HEAD, RWKV_HEAD), F32)
    names = ("pk0", "pk1", "pk2", "pv0", "pv1", "pv2", "pwkv", "pshift",
             "sk0", "sk1", "sk2", "sv0", "sv1", "sv2", "swkv", "sshift")
    outs = {k_: [] for k_ in names}
    rope_s = _rope_tables(PAST_LEN + jnp.arange(Ts, dtype=jnp.int32))
    assert att_cols + ZW <= w_in.shape[2]
    wb = {"w_in": w_in.astype(BF16), "w_gate": w_in[:, :, att_cols + rwkv_cols:].astype(BF16),
          "w_att_o": w_att_o.astype(BF16), "w_rwkv_o": w_rwkv_o.astype(BF16), "w_out": w_out.astype(BF16),
          "w_ffn_up": w_ffn_up.astype(BF16), "w_ffn_down": w_ffn_down.astype(BF16)}
    for l in range(DEPTH):
        yp, kn_p, vn_p, z_p, wkv_p = _layer(yp, l, wb, wts, rope_p, B=B, T=S, caches=None,
                                            wkv0=zeros_wkv, shift0=None)
        ys, kn_s, vn_s, z_s, wkv_s = _layer(ys, l, wb, wts, rope_s, B=Bs, T=1, caches=caches,
                                            wkv0=state_wkv[l], shift0=state_shift[l])
        for gi in range(N_GROUPS):
            outs[f"pk{gi}"].append(kn_p[gi])
            outs[f"pv{gi}"].append(vn_p[gi])
            outs[f"sk{gi}"].append(kn_s[gi])
            outs[f"sv{gi}"].append(vn_s[gi])
        outs["pwkv"].append(wkv_p)
        outs["swkv"].append(wkv_s)
        outs["pshift"].append(z_p.reshape(B, S, ZW)[:, -1, :rwkv_cols])
        outs["sshift"].append(z_s[:Bs, :rwkv_cols])
    st = lambda n: jnp.stack(outs[n])
    new_kv_p = [jnp.stack([st(f"pk{gi}"), st(f"pv{gi}")], axis=3) for gi in range(N_GROUPS)]
    new_kv_s = [_shift_append(caches[gi], jnp.stack([st(f"sk{gi}"), st(f"sv{gi}")], axis=3))
                for gi in range(N_GROUPS)]
    return (yp.reshape(B, S, D), ys[:Bs].reshape(Bs, 1, D), *new_kv_p, st("pwkv"), st("pshift"),
            *new_kv_s, st("swkv"), st("sshift"))
```

```python
import functools
import math

import jax
import jax.numpy as jnp
from jax import lax
from jax.experimental import pallas as pl
from jax.experimental.pallas import tpu as pltpu

F32 = jnp.float32
BF16 = jnp.bfloat16

N_GROUPS = 3
ATT_GROUPS = ((128, 1), (512, 4), (2048, 16))
HEAD_DIM = 128
RWKV_HEAD = 64
LORA_W = 128
LORA_A = 128
LORA_G = 480
ROPE_THETA = 10000.0
PAST_LEN = 8192
NEG_INF = -1e30
GN_EPS = 64e-5
NORM_EPS = 1e-6

LANES = 128
VMEM_LIMIT_BYTES = 56 * 1024 * 1024

CHUNK = 64
LORA_G_PAD = 512
SPAN = 128


def _cparams(sem):
    return pltpu.CompilerParams(dimension_semantics=sem, vmem_limit_bytes=VMEM_LIMIT_BYTES)


def _pick(n, prefs):
    for p in prefs:
        if n % p == 0:
            return p
    return n


def _split(x, n):
    parts, rem = [], x
    for i in range(n):
        p = rem.astype(BF16)
        parts.append(p)
        if i + 1 < n:
            rem = rem - p.astype(F32)
    return parts


_NN = (((1,), (0,)), ((), ()))
_NT = (((1,), (1,)), ((), ()))
_TN = (((0,), (0,)), ((), ()))


def _mmp(a_parts, b_parts, dims=_NN):
    order = max(len(a_parts), len(b_parts))
    acc = None
    for i, a in enumerate(a_parts):
        for j, b in enumerate(b_parts):
            if i + j < order:
                t = lax.dot_general(a, b, dims, preferred_element_type=F32)
                acc = t if acc is None else acc + t
    return acc


def _rms_kernel(x_ref, g_ref, o_ref):
    x = x_ref[...]
    ms = jnp.mean(x * x, axis=-1, keepdims=True)
    o_ref[...] = (x * lax.rsqrt(ms + NORM_EPS) * g_ref[...]).astype(o_ref.dtype)


def _rmsnorm(x, g, l):
    M, D = x.shape
    tm = _pick(M, (256, 128, 64, 32, 16, 8))
    return pl.pallas_call(
        _rms_kernel,
        grid=(M // tm,),
        in_specs=[pl.BlockSpec((tm, D), lambda i: (i, 0)),
                  pl.BlockSpec((None, 1, D), lambda i: (l, 0, 0))],
        out_specs=pl.BlockSpec((tm, D), lambda i: (i, 0)),
        out_shape=jax.ShapeDtypeStruct((M, D), BF16),
        compiler_params=_cparams(("parallel",)),
        name="rmsnorm",
    )(x, g.reshape(g.shape[0], 1, D))


def _ep_none(acc):
    return acc


def _ep_relu2(acc):
    return jnp.square(jnp.maximum(acc, 0.0))


def _ep_gate(acc, g):
    return jax.nn.sigmoid(g) * acc


def _ep_gate_add(acc, g, t):
    return t + jax.nn.sigmoid(g) * acc


def _ep_residual(acc, x):
    return x + acc


CAST_CHUNK = 512


def _tile_dots(a_refs, w_ref, w_t):
    dims = _NT if w_t else _NN
    if w_ref.dtype == BF16:
        wb = w_ref[...]
        return [lax.dot_general(a[...], wb, dims, preferred_element_type=F32) for a in a_refs]
    tk = w_ref.shape[1 if w_t else 0]
    kc = min(CAST_CHUNK, tk)
    accs = [None] * len(a_refs)
    for c in range(tk // kc):
        ks = slice(c * kc, (c + 1) * kc)
        wb = (w_ref[:, ks] if w_t else w_ref[ks, :]).astype(BF16)
        for n, a in enumerate(a_refs):
            t = lax.dot_general(a[:, ks], wb, dims, preferred_element_type=F32)
            accs[n] = t if accs[n] is None else accs[n] + t
    return accs


def _mm_kernel(*refs, nk, n_extra, epilogue, n_groups, w_t):
    w_ref = refs[0]
    per = 1 + n_extra
    groups = [refs[1 + g * per:1 + (g + 1) * per] for g in range(n_groups)]
    outs = refs[1 + n_groups * per:1 + n_groups * per + n_groups]
    accs = refs[1 + n_groups * per + n_groups:]
    a_refs = [g[0] for g in groups]

    def finish(vals):
        for o_ref, g, acc in zip(outs, groups, vals):
            o_ref[...] = epilogue(acc, *[e[...] for e in g[1:]]).astype(o_ref.dtype)

    if nk == 1:
        finish(_tile_dots(a_refs, w_ref, w_t))
        return
    k = pl.program_id(2)

    @pl.when(k == 0)
    def _():
        for acc_ref in accs:
            acc_ref[...] = jnp.zeros_like(acc_ref)

    for acc_ref, t in zip(accs, _tile_dots(a_refs, w_ref, w_t)):
        acc_ref[...] += t

    @pl.when(k == nk - 1)
    def _():
        finish([acc_ref[...] for acc_ref in accs])


def _matmul(a, w, l, *, out_dtype, n_cols=None, w_col0=0, epilogue=_ep_none, extras=(),
            small=None, w_t=False, tm=None, tn=None, tk=None, name="matmul"):
    M, K = a.shape
    N = n_cols or w.shape[1 if w_t else 2]
    tm = tm or _pick(M, (1024, 512, 256, 128, 64, 32, 16))
    tn_prefs = (1024, 512, 256, 128) if w.dtype == BF16 else (512, 256, 128)
    tn = tn or _pick(math.gcd(N, w_col0) if w_col0 else N, tn_prefs)
    tk = tk or _pick(K, (4096, 2048, 1024, 512, 256, 128))
    nk = K // tk
    assert w_col0 % tn == 0 and N % tn == 0 and M % tm == 0
    if w_t:
        in_specs = [pl.BlockSpec((None, tn, tk), lambda i, j, k, _o=w_col0 // tn: (l, j + _o, k))]
    else:
        in_specs = [pl.BlockSpec((None, tk, tn), lambda i, j, k, _o=w_col0 // tn: (l, k, j + _o))]
    args = [w]
    row_groups = [(a, extras, tm, lambda i: i)] + ([(small[0], small[1], small[0].shape[0], lambda i: 0)]
                                                    if small is not None else [])
    out_specs, out_shapes, scratch = [], [], []
    for arr, ex, rows, row_of in row_groups:
        assert len(ex) == len(extras)
        in_specs.append(pl.BlockSpec((rows, tk), lambda i, j, k, _r=row_of: (_r(i), k)))
        args.append(arr)
        for e_arr, off in ex:
            assert off % tn == 0
            in_specs.append(pl.BlockSpec((rows, tn), lambda i, j, k, _o=off // tn, _r=row_of: (_r(i), j + _o)))
            args.append(e_arr)
        out_specs.append(pl.BlockSpec((rows, tn), lambda i, j, k: (i, j)))
        out_shapes.append(jax.ShapeDtypeStruct(((M // tm) * rows, N), out_dtype))
        if nk > 1:
            scratch.append(pltpu.VMEM((rows, tn), F32))
    sem = ("parallel", "parallel", "arbitrary")
    res = pl.pallas_call(
        functools.partial(_mm_kernel, nk=nk, n_extra=len(extras), epilogue=epilogue,
                          n_groups=len(row_groups), w_t=w_t),
        grid=(M // tm, N // tn, nk),
        in_specs=in_specs,
        out_specs=out_specs,
        out_shape=out_shapes,
        scratch_shapes=scratch,
        compiler_params=_cparams(sem),
        name=name,
    )(*args)
    return res[0] if small is None else (res[0], res[1][:small[0].shape[0]])


def _mm_ws_kernel(*refs, n_extra, epilogue, has_small, cast):
    w_ref, a_ref = refs[0], refs[1]
    extra = refs[2:2 + n_extra]
    p = 2 + n_extra
    if has_small:
        a2_ref, extra2 = refs[p], refs[p + 1:p + 1 + n_extra]
        p += 1 + n_extra
    o_ref = refs[p]
    o2_ref = refs[p + 1] if has_small else None
    wb_ref = refs[-1] if cast else w_ref

    @pl.when(pl.program_id(1) == 0)
    def _():
        if cast:
            tk = w_ref.shape[0]
            kc = min(CAST_CHUNK, tk)
            for c in range(tk // kc):
                wb_ref[c * kc:(c + 1) * kc, :] = w_ref[c * kc:(c + 1) * kc, :].astype(BF16)
        if has_small:
            acc2 = jnp.dot(a2_ref[...], wb_ref[...], preferred_element_type=F32)
            o2_ref[...] = epilogue(acc2, *[e[...] for e in extra2]).astype(o2_ref.dtype)

    acc = jnp.dot(a_ref[...], wb_ref[...], preferred_element_type=F32)
    o_ref[...] = epilogue(acc, *[e[...] for e in extra]).astype(o_ref.dtype)


def _matmul_ws(a, w, l, *, out_dtype, n_cols=None, w_col0=0, epilogue=_ep_none, extras=(),
               small=None, tm=None, tn=None, name="matmul"):
    M, K = a.shape
    N = n_cols or w.shape[2]
    cast = w.dtype != BF16
    tm = tm or _pick(M, (1024, 512, 256, 128, 64, 32, 16))
    tn_prefs = (512, 256, 128) if cast else (1024, 512, 256, 128)
    tn = tn or _pick(math.gcd(N, w_col0) if w_col0 else N, tn_prefs)
    assert w_col0 % tn == 0 and N % tn == 0 and M % tm == 0
    in_specs = [pl.BlockSpec((None, K, tn), lambda j, i, _o=w_col0 // tn: (l, 0, j + _o))]
    args = [w]
    row_groups = [(a, extras, tm, lambda i: i)]
    if small is not None:
        row_groups.append((small[0], small[1], small[0].shape[0], lambda i: 0))
    out_specs, out_shapes = [], []
    for arr, ex, rows, row_of in row_groups:
        assert len(ex) == len(extras)
        in_specs.append(pl.BlockSpec((rows, K), lambda j, i, _r=row_of: (_r(i), 0)))
        args.append(arr)
        for e_arr, off in ex:
            assert off % tn == 0
            in_specs.append(pl.BlockSpec((rows, tn), lambda j, i, _o=off // tn, _r=row_of: (_r(i), j + _o)))
            args.append(e_arr)
        out_specs.append(pl.BlockSpec((rows, tn), lambda j, i, _r=row_of: (_r(i), j)))
        out_shapes.append(jax.ShapeDtypeStruct((arr.shape[0], N), out_dtype))
    res = pl.pallas_call(
        functools.partial(_mm_ws_kernel, n_extra=len(extras), epilogue=epilogue,
                          has_small=small is not None, cast=cast),
        grid=(N // tn, M // tm),
        in_specs=in_specs,
        out_specs=out_specs,
        out_shape=out_shapes,
        scratch_shapes=[pltpu.VMEM((K, tn), BF16)] if cast else [],
        compiler_params=_cparams(("parallel", "arbitrary")),
        name=name,
    )(*args)
    return res[0] if small is None else tuple(res)


def _norm_rope(x, g, cos, sin):
    ms = jnp.mean(x * x, axis=-1, keepdims=True)
    y = x * lax.rsqrt(ms + NORM_EPS) * g
    return y * cos + pltpu.roll(y, HEAD_DIM // 2, 1) * sin


ROW_CHUNK = 256


def _band_attn_kernel(q0, k0, v0, q1, k1, v1, q2, k2, v2, cos_ref, sin_ref, qg_ref, kg_ref,
                      att_ref, kn0, kn1, kn2, vn0, vn1, vn2, qs, ks, os_, ls, *, S):
    scale = HEAD_DIM ** -0.5
    qkv = ((q0, k0, v0, kn0, vn0), (q1, k1, v1, kn1, vn1), (q2, k2, v2, kn2, vn2))
    for gi, (q_ref, k_ref, v_ref, kn_ref, vn_ref) in enumerate(qkv):
        win, dil = ATT_GROUPS[gi]
        L = S // dil
        for c in range(S // ROW_CHUNK):
            sl = slice(c * ROW_CHUNK, (c + 1) * ROW_CHUNK)
            cos, sin = cos_ref[sl, :], sin_ref[sl, :]
            qs[sl, :] = _norm_rope(q_ref[0, sl, :], qg_ref[...], cos, sin)
            ks[sl, :] = _norm_rope(k_ref[0, sl, :], kg_ref[...], cos, sin)
        kn_ref[0] = ks[S - win:, :]
        vn_ref[0] = v_ref[0, S - win:, :]

        def rows(r, start, n, dil=dil):
            return pl.ds(start, n) if dil == 1 else pl.ds(r + start * dil, n, stride=dil)

        for r in range(dil):
            for i in range(L // SPAN):
                lo = max(i - 1, 0) * SPAN
                nkeys = (i + 1) * SPAN - lo
                qb = qs[rows(r, i * SPAN, SPAN), :].astype(BF16)
                kb = ks[rows(r, lo, nkeys), :].astype(BF16)
                vb = v_ref[0, rows(r, lo, nkeys), :].astype(BF16)
                s = lax.dot_general(qb, kb, _NT, preferred_element_type=F32) * scale
                qi = lax.broadcasted_iota(jnp.int32, (SPAN, nkeys), 0) + (i * SPAN - lo)
                ki = lax.broadcasted_iota(jnp.int32, (SPAN, nkeys), 1)
                s = jnp.where((ki <= qi) & (ki >= qi - SPAN), s, NEG_INF)
                m = jnp.max(s, axis=-1, keepdims=True)
                p = jnp.exp(s - m)
                den = jnp.sum(p, axis=-1, keepdims=True)
                o = jnp.dot(p.astype(BF16), vb, preferred_element_type=F32)
                os_[gi, rows(r, i * SPAN, SPAN), :] = o / den
                ls[gi, rows(r, i * SPAN, SPAN), :] = jnp.broadcast_to(m + jnp.log(den), (SPAN, HEAD_DIM))
    for c in range(S // ROW_CHUNK):
        sl = slice(c * ROW_CHUNK, (c + 1) * ROW_CHUNK)
        a0, a1, a2 = ls[0, sl, :], ls[1, sl, :], ls[2, sl, :]
        mx = jnp.maximum(jnp.maximum(a0, a1), a2)
        e0, e1, e2 = jnp.exp(a0 - mx), jnp.exp(a1 - mx), jnp.exp(a2 - mx)
        num = e0 * os_[0, sl, :] + e1 * os_[1, sl, :] + e2 * os_[2, sl, :]
        att_ref[0, sl, :] = (num / (e0 + e1 + e2)).astype(att_ref.dtype)


def _band_attention(proj, cos, sin, qg, kg, l, *, B, S, H):
    NP = proj.shape[1]
    G = N_GROUPS
    W = H * HEAD_DIM
    for win, dil in ATT_GROUPS:
        assert win // dil == SPAN and (S // dil) % SPAN == 0 and S % ROW_CHUNK == 0 and win <= S
    pv = proj.reshape(B, S, NP)
    qkv_specs = [pl.BlockSpec((1, S, HEAD_DIM), lambda b, h, _o=(c * G + gi) * H: (b, 0, _o + h))
                 for gi in range(G) for c in range(3)]
    tab_spec = pl.BlockSpec((S, HEAD_DIM), lambda b, h: (0, 0))
    g_spec = pl.BlockSpec((None, 1, HEAD_DIM), lambda b, h: (l, 0, 0))
    win_specs = [pl.BlockSpec((1, win, HEAD_DIM), lambda b, h: (b, 0, h)) for win, _ in ATT_GROUPS]
    win_shapes = [jax.ShapeDtypeStruct((B, win, W), F32) for win, _ in ATT_GROUPS]
    outs = pl.pallas_call(
        functools.partial(_band_attn_kernel, S=S),
        grid=(B, H),
        in_specs=qkv_specs + [tab_spec, tab_spec, g_spec, g_spec],
        out_specs=[pl.BlockSpec((1, S, HEAD_DIM), lambda b, h: (b, 0, h))] + win_specs + win_specs,
        out_shape=[jax.ShapeDtypeStruct((B, S, W), BF16)] + win_shapes + win_shapes,
        scratch_shapes=[pltpu.VMEM((S, HEAD_DIM), F32), pltpu.VMEM((S, HEAD_DIM), F32),
                        pltpu.VMEM((G, S, HEAD_DIM), F32), pltpu.VMEM((G, S, HEAD_DIM), F32)],
        compiler_params=_cparams(("parallel", "parallel")),
        name="band_attn",
    )(*([pv] * (3 * G)), cos, sin, qg.reshape(-1, 1, HEAD_DIM), kg.reshape(-1, 1, HEAD_DIM))
    att = outs[0].reshape(B * S, W)
    heads = lambda t: t.reshape(B, t.shape[1], H, HEAD_DIM)
    return att, [heads(t) for t in outs[1:1 + G]], [heads(t) for t in outs[1 + G:]]


def _cached_attn_kernel(p_ref, c0_ref, c1_ref, c2_ref, cos_ref, sin_ref, qg_ref, kg_ref,
                        att_ref, knew_ref, *, H):
    cos = cos_ref[...]
    sin = sin_ref[...]
    scale = HEAD_DIM ** -0.5
    G = N_GROUPS
    W = H * HEAD_DIM
    caches = (c0_ref, c1_ref, c2_ref)
    for h in range(H):
        ms, dens, outs = [], [], []
        for gi in range(G):
            def col(c):
                o = ((c * G + gi) * H + h) * HEAD_DIM
                return p_ref[0, :, o:o + HEAD_DIM]
            q = _norm_rope(col(0), qg_ref[...], cos, sin)
            k = _norm_rope(col(1), kg_ref[...], cos, sin)
            v = col(2)
            knew_ref[0, :, (gi * H + h) * HEAD_DIM:(gi * H + h + 1) * HEAD_DIM] = k
            kc = caches[gi][0, 0, :, 0, 0, h, :]
            vc = caches[gi][0, 0, :, 0, 1, h, :]
            q8 = jnp.broadcast_to(q, (8, HEAD_DIM)).astype(BF16)
            s_c = lax.dot_general(q8, kc.astype(BF16), _NT, preferred_element_type=F32)[0:1] * scale
            s_n = jnp.sum(q.astype(BF16).astype(F32) * k.astype(BF16).astype(F32),
                          axis=-1, keepdims=True) * scale
            m = jnp.maximum(jnp.max(s_c, axis=-1, keepdims=True), s_n)
            p_c = jnp.exp(s_c - m)
            p_n = jnp.exp(s_n - m)
            den = jnp.sum(p_c, axis=-1, keepdims=True) + p_n
            p8 = jnp.broadcast_to(p_c, (8, SPAN)).astype(BF16)
            o = jnp.dot(p8, vc.astype(BF16), preferred_element_type=F32)[0:1]
            o = o + p_n.astype(BF16).astype(F32) * v.astype(BF16).astype(F32)
            ms.append(m)
            dens.append(den)
            outs.append(o)
        mx = jnp.maximum(jnp.maximum(ms[0], ms[1]), ms[2])
        num = jnp.zeros((1, HEAD_DIM), F32)
        tot = jnp.zeros((1, 1), F32)
        for gi in range(G):
            e = jnp.exp(ms[gi] - mx)
            num = num + e * outs[gi]
            tot = tot + e * dens[gi]
        att_ref[0, :, h * HEAD_DIM:(h + 1) * HEAD_DIM] = (num / tot).astype(att_ref.dtype)


def _cached_attention(proj_s, caches, cos, sin, qg, kg, l, *, B, H):
    NP = proj_s.shape[-1]
    W = H * HEAD_DIM
    views, specs = [], []
    for gi, (win, dil) in enumerate(ATT_GROUPS):
        c = caches[gi]
        Lb = c.shape[2]
        assert Lb == win and Lb // dil == SPAN
        views.append(c.reshape(c.shape[0], B, SPAN, dil, 2, H, HEAD_DIM))
        specs.append(pl.BlockSpec((1, 1, SPAN, 1, 2, H, HEAD_DIM), lambda b: (l, b, 0, 0, 0, 0, 0)))
    tab = pl.BlockSpec((1, HEAD_DIM), lambda b: (0, 0))
    gsp = pl.BlockSpec((None, 1, HEAD_DIM), lambda b: (l, 0, 0))
    return pl.pallas_call(
        functools.partial(_cached_attn_kernel, H=H),
        grid=(B,),
        in_specs=[pl.BlockSpec((1, 1, NP), lambda b: (b, 0, 0))] + specs + [tab, tab, gsp, gsp],
        out_specs=[pl.BlockSpec((1, 1, W), lambda b: (b, 0, 0)),
                   pl.BlockSpec((1, 1, N_GROUPS * W), lambda b: (b, 0, 0))],
        out_shape=[jax.ShapeDtypeStruct((B, 1, W), F32),
                   jax.ShapeDtypeStruct((B, 1, N_GROUPS * W), F32)],
        compiler_params=_cparams(("parallel",)),
        name="cached_attn",
    )(proj_s, *views, cos, sin, qg.reshape(-1, 1, HEAD_DIM), kg.reshape(-1, 1, HEAD_DIM))


def _shift_append_kernel(c_ref, nxt_ref, new_ref, o_ref, *, R, n_tiles):
    j = pl.program_id(2)
    o_ref[0, 0, 0:R - 1] = c_ref[0, 0, 1:R]
    o_ref[0, 0, R - 1] = jnp.where(j == n_tiles - 1, new_ref[0, 0, 0], nxt_ref[0, 0, 0])


def _shift_append(cache, new_rows):
    DEPTH, B, Lb = cache.shape[:3]
    tail = cache.shape[3:]
    R = _pick(Lb, (256, 128, 64, 32, 16, 8))
    n_tiles = Lb // R
    z = (0,) * len(tail)
    return pl.pallas_call(
        functools.partial(_shift_append_kernel, R=R, n_tiles=n_tiles),
        grid=(DEPTH, B, n_tiles),
        in_specs=[pl.BlockSpec((1, 1, R) + tail, lambda l, b, j: (l, b, j) + z),
                  pl.BlockSpec((1, 1, 1) + tail, lambda l, b, j: (l, b, jnp.minimum((j + 1) * R, Lb - 1)) + z),
                  pl.BlockSpec((1, 1, 1) + tail, lambda l, b, j: (l, b, 0) + z)],
        out_specs=pl.BlockSpec((1, 1, R) + tail, lambda l, b, j: (l, b, j) + z),
        out_shape=jax.ShapeDtypeStruct(cache.shape, cache.dtype),
        compiler_params=_cparams(("parallel", "parallel", "parallel")),
        name="shift_append",
    )(cache, cache, new_rows)


def _head_sum(x, jmat):
    return _mmp(_split(x, 2), [jmat])


def _rwkv_prep_kernel(z_ref, zp_ref, mu_ref, w0_ref, a0_ref, kk_ref, ka_ref,
                      du_ref, iu_ref, gu_ref,
                      r_ref, k_ref, v_ref, ap_ref, b_ref, ld_ref, g_ref,
                      *, RW, single_step, tiles_per_seq):
    z = z_ref[...]
    if single_step:
        zp = zp_ref[...]
    else:
        first = pl.program_id(0) % tiles_per_seq == 0
        prev_row = jnp.where(first, 0.0, zp_ref[7:8, :])
        rows = lax.broadcasted_iota(jnp.int32, z.shape, 0)
        zp = jnp.where(rows == 0, prev_row, pltpu.roll(z, 1, 0))
    zs = z + mu_ref[...] * (zp - z)
    r = zs[:, 0:RW]
    k = zs[:, RW:2 * RW]
    v = zs[:, 2 * RW:3 * RW]
    o = 3 * RW
    w_lo = zs[:, o:o + LORA_W]
    a_lo = zs[:, o + LORA_W:o + LORA_W + LORA_A]
    g_lo = zs[:, o + LORA_W + LORA_A:o + LORA_W + LORA_A + LORA_G_PAD]

    def lora(x, u_ref):
        return _mmp(_split(x, 2), [u_ref[0], u_ref[1]])

    xw = w0_ref[...] + lora(jnp.tanh(w_lo), du_ref)
    ld_ref[...] = -math.exp(-0.5) * jax.nn.sigmoid(xw)
    a = jax.nn.sigmoid(a0_ref[...] + lora(a_lo, iu_ref))
    g_ref[...] = lora(jax.nn.sigmoid(g_lo), gu_ref)
    r_ref[...] = r
    v_ref[...] = v
    k_ref[...] = k * (1.0 + (a - 1.0) * ka_ref[...])
    ri = lax.broadcasted_iota(jnp.int32, (LANES, LANES), 0) // RWKV_HEAD
    ci = lax.broadcasted_iota(jnp.int32, (LANES, LANES), 1) // RWKV_HEAD
    jmat = jnp.where(ri == ci, 1.0, 0.0).astype(BF16)
    for t in range(RW // LANES):
        sl = slice(t * LANES, (t + 1) * LANES)
        kk = k[:, sl] * kk_ref[:, sl]
        ss = _head_sum(kk * kk, jmat)
        kk = kk * lax.rsqrt(jnp.maximum(ss, 1e-24))
        ap_ref[:, sl] = -kk
        b_ref[:, sl] = kk * a[:, sl]


def _rwkv_prep(proj, zprev, l, wts, *, rows_per_seq, single_step):
    M = proj.shape[0]
    RW = wts["decay_w0"].shape[-1]
    ZW = wts["mu_p"].shape[-1]
    tm = 8 if single_step else _pick(rows_per_seq, (128, 64, 32, 16, 8))
    z_spec = pl.BlockSpec((tm, ZW), lambda i: (i, 0))
    if single_step:
        zp_arr, zp_spec = zprev, pl.BlockSpec((tm, ZW), lambda i: (i, 0))
    else:
        zp_arr = proj
        zp_spec = pl.BlockSpec((8, ZW), lambda i: (jnp.maximum(i * (tm // 8) - 1, 0), 0))

    def vec(n):
        return pl.BlockSpec((None, 1, n), lambda i: (l, 0, 0))

    def up(kdim):
        return pl.BlockSpec((None, 2, kdim, RW), lambda i: (l, 0, 0, 0))

    o_spec = pl.BlockSpec((tm, RW), lambda i: (i, 0))
    outs = pl.pallas_call(
        functools.partial(_rwkv_prep_kernel, RW=RW, single_step=single_step,
                          tiles_per_seq=max(rows_per_seq // tm, 1)),
        grid=(M // tm,),
        in_specs=[z_spec, zp_spec, vec(ZW), vec(RW), vec(RW), vec(RW), vec(RW),
                  up(LORA_W), up(LORA_A), up(LORA_G_PAD)],
        out_specs=[o_spec] * 7,
        out_shape=[jax.ShapeDtypeStruct((M, RW), F32)] * 7,
        compiler_params=_cparams(("parallel",)),
        name="rwkv_prep",
    )(proj, zp_arr, wts["mu_p"], wts["decay_w0"], wts["iclr_a0"], wts["key_k"], wts["key_a"],
      wts["decay_up2"], wts["iclr_up2"], wts["gate_up2"])
    return outs


WKV_PARTS = 1
INV_BASE = 4


def _mm(a, b, dims=_NN, parts=WKV_PARTS):
    return _mmp(_split(a, parts), _split(b, parts), dims)


def _each(f, *lists):
    return [f(*xs) for xs in zip(*lists)]


def _wkv_pairs(r, k, v, ap, b, ld, wt, consts):
    C = CHUNK
    C2 = 2 * C
    tri, m2, strict, incl, eye = consts
    cat0 = lambda *xs: jnp.concatenate(xs, axis=0)
    lc = _each(lambda x: _mmp([tri], _split(x, 3)), ld)
    lt = [x[C - 1:C, :] for x in lc]

    def stack(x):
        return jnp.where(m2, jnp.concatenate([x, x], axis=0), 0.0)

    ar2 = _each(lambda ap_, r_, lc_, ld_: cat0(stack(ap_ * jnp.exp(lc_ - ld_)), stack(r_ * jnp.exp(lc_))),
                ap, r, lc, ld)
    bk2 = _each(lambda b_, k_, lc_: cat0(stack(b_ * jnp.exp(-lc_)), stack(k_ * jnp.exp(-lc_))), b, k, lc)
    bke = _each(lambda b_, k_, lc_, lt_: cat0(stack(b_ * jnp.exp(lt_ - lc_)), stack(k_ * jnp.exp(lt_ - lc_))),
                b, k, lc, lt)
    v2 = _each(stack, v)
    a4 = _each(lambda x_, y_: _mm(x_, y_, _NT), ar2, bk2)
    n_ab = [jnp.where(strict, a[0:C2, 0:C2], 0.0) for a in a4]
    a_akrk = [cat0(jnp.where(strict, a[0:C2, C2:], 0.0), jnp.where(incl, a[C2:, C2:], 0.0)) for a in a4]
    a_rb = [jnp.where(incl, a[C2:, 0:C2], 0.0) for a in a4]
    sw = _each(lambda x_, w_: _mm(x_, w_, _NT), ar2, wt)
    av = _each(_mm, a_akrk, v2)
    bi = lax.broadcasted_iota(jnp.int32, (C2, C2), 0)
    bj = lax.broadcasted_iota(jnp.int32, (C2, C2), 1)
    n0 = [jnp.where((bi // INV_BASE) == (bj // INV_BASE), n, 0.0) for n in n_ab]
    x = _each(lambda n_, q_: (eye + n_) + _mm(q_, eye + n_), n0, _each(_mm, n0, n0))
    blk = INV_BASE
    while blk < C:
        off_mask = ((bi // (2 * blk)) == (bj // (2 * blk))) & ((bi // blk) != (bj // blk))
        t_off = _each(lambda t_, n_: _mm(t_, jnp.where(off_mask, n_, 0.0)), x, n_ab)
        x = _each(lambda t_, to_: t_ + _mm(to_, t_), x, t_off)
        blk *= 2
    u2 = _each(lambda x_, s_, a_: _mm(x_, s_[0:C2] + a_[0:C2]), x, sw, av)
    y2 = _each(lambda s_, a_, rb_, u_: s_[C2:] + a_[C2:] + _mm(rb_, u_), sw, av, a_rb, u2)
    y = [t[0:C] + t[C:C2] for t in y2]
    wt_new = _each(lambda w_, lt_, u_, v_, e_: w_ * jnp.exp(lt_) + _mm(cat0(u_, v_), e_, _TN),
                   wt, lt, u2, v2, bke)
    return y, wt_new


def _wkv_kernel(r_ref, k_ref, v_ref, ap_ref, b_ref, ld_ref, g_ref, s0_ref,
                brk_ref, lg_ref, lb_ref, y_ref, sout_ref, wt_scr, *, n_pairs, n_chunks):
    C = CHUNK
    c = pl.program_id(2)
    HD = RWKV_HEAD

    @pl.when(c == 0)
    def _():
        wt_scr[...] = jnp.zeros_like(wt_scr)
        for pi in range(n_pairs):
            wt_scr[pi, 0:HD, 0:HD] = s0_ref[0, 2 * pi]
            wt_scr[pi, HD:2 * HD, HD:2 * HD] = s0_ref[0, 2 * pi + 1]

    ri = lax.broadcasted_iota(jnp.int32, (2 * C, LANES), 0)
    ci = lax.broadcasted_iota(jnp.int32, (2 * C, LANES), 1)
    tri_r = lax.broadcasted_iota(jnp.int32, (C, C), 0)
    tri_c = lax.broadcasted_iota(jnp.int32, (C, C), 1)
    tri = jnp.where(tri_c <= tri_r, 1.0, 0.0).astype(BF16)
    m2 = (ri // C) == (ci // HD)
    same = (ri // C) == (ci // C)
    strict = same & ((ci % C) < (ri % C))
    incl = same & ((ci % C) <= (ri % C))
    eye = jnp.where(ri == ci, 1.0, 0.0)
    jmat = jnp.where((ri // HD) == (ci // HD), 1.0, 0.0).astype(BF16)
    consts = (tri, m2, strict, incl, eye)
    inv = 1.0 / HD
    sls = [slice(pi * LANES, (pi + 1) * LANES) for pi in range(n_pairs)]
    col = lambda ref: [ref[0, :, sl] for sl in sls]
    r, k, v = col(r_ref), col(k_ref), col(v_ref)
    y, wt_new = _wkv_pairs(r, k, v, col(ap_ref), col(b_ref), col(ld_ref),
                           [wt_scr[pi] for pi in range(n_pairs)], consts)
    for pi in range(n_pairs):
        wt_scr[pi] = wt_new[pi]
    mu = [_head_sum(t, jmat) * inv for t in y]
    d = _each(lambda y_, m_: y_ - m_, y, mu)
    var = [_head_sum(t * t, jmat) * inv for t in d]
    rk = _each(lambda r_, k_, sl: _head_sum(r_ * k_ * brk_ref[:, sl], jmat), r, k, sls)
    for pi, sl in enumerate(sls):
        yn = d[pi] * lax.rsqrt(var[pi] + GN_EPS) * lg_ref[:, sl] + lb_ref[:, sl]
        y_ref[0, :, sl] = ((yn + rk[pi] * v[pi]) * g_ref[0, :, sl]).astype(y_ref.dtype)

    @pl.when(c == n_chunks - 1)
    def _():
        for pi in range(n_pairs):
            sout_ref[0, 2 * pi] = wt_scr[pi, 0:HD, 0:HD]
            sout_ref[0, 2 * pi + 1] = wt_scr[pi, HD:2 * HD, HD:2 * HD]


def _wkv(r, k, v, ap, b, ld, g, s0, l, wts, *, B, T):
    RW = r.shape[-1]
    NH = RW // RWKV_HEAD
    WT = _pick(RW, (1024, 512, 256, 128))
    n_pairs = WT // LANES
    n_chunks = T // CHUNK
    x_spec = pl.BlockSpec((1, CHUNK, WT), lambda bb, j, c: (bb, c, j))
    s_spec = pl.BlockSpec((1, 2 * n_pairs, RWKV_HEAD, RWKV_HEAD), lambda bb, j, c: (bb, j, 0, 0))
    p_spec = pl.BlockSpec((None, 1, WT), lambda bb, j, c: (l, 0, j))
    sh = lambda t: t.reshape(B, T, RW)
    y, s_new = pl.pallas_call(
        functools.partial(_wkv_kernel, n_pairs=n_pairs, n_chunks=n_chunks),
        grid=(B, RW // WT, n_chunks),
        in_specs=[x_spec] * 7 + [s_spec, p_spec, p_spec, p_spec],
        out_specs=[x_spec, s_spec],
        out_shape=[jax.ShapeDtypeStruct((B, T, RW), BF16),
                   jax.ShapeDtypeStruct((B, NH, RWKV_HEAD, RWKV_HEAD), F32)],
        scratch_shapes=[pltpu.VMEM((n_pairs, LANES, LANES), F32)],
        compiler_params=_cparams(("parallel", "parallel", "arbitrary")),
        name="wkv_chunk",
    )(sh(r), sh(k), sh(v), sh(ap), sh(b), sh(ld), sh(g), s0,
      wts["bonus_rk"], wts["lnx_g"], wts["lnx_b"])
    return y.reshape(B * T, RW), s_new


def _layer(xp, xs, l, wb, wts, rope_p, rope_s, *, B, S, Bs, caches, wkv0_s, shift0_s):
    D = xp.shape[1]
    Ms = xs.shape[0]
    W = wb["w_att_o"].shape[1]
    H = W // HEAD_DIM
    NH = wts["decay_w0"].shape[-1] // RWKV_HEAD
    ZW = wts["mu_p"].shape[-1]
    att_cols = 3 * N_GROUPS * W

    def mm(a_p, a_s, w, ex_p=(), ex_s=(), **kw):
        return _matmul_ws(a_p, w, l, extras=ex_p, small=(a_s, ex_s), **kw)

    xn_p = _rmsnorm(xp, wts["norm1_g"], l)
    xn_s = _rmsnorm(xs, wts["norm1_g"], l)
    pa_p, pa_s = mm(xn_p, xn_s, wb["w_in"], n_cols=att_cols, out_dtype=F32, name="in_proj_att")
    pz_p, pz_s = mm(xn_p, xn_s, wb["w_in"], n_cols=ZW, w_col0=att_cols, out_dtype=F32, name="in_proj_rwkv")
    pg_p, pg_s = mm(xn_p, xn_s, wb["w_gate"], out_dtype=F32, name="in_proj_gate")
    att_p, kn_p, vn_p = _band_attention(pa_p, *rope_p, wts["q_norm_g"], wts["k_norm_g"], l, B=B, S=S, H=H)
    att_s, kn = _cached_attention(pa_s[:Bs].reshape(Bs, 1, -1), caches, *rope_s,
                                  wts["q_norm_g"], wts["k_norm_g"], l, B=Bs, H=H)
    att_s = jnp.pad(att_s.reshape(Bs, W).astype(BF16), ((0, Ms - Bs), (0, 0)))
    kn = kn.reshape(Bs, 1, N_GROUPS, H, HEAD_DIM)
    kn_s = [kn[:, :, gi] for gi in range(N_GROUPS)]
    vv = pa_s[:Bs, 2 * N_GROUPS * W:].reshape(Bs, 1, N_GROUPS, H, HEAD_DIM)
    vn_s = [vv[:, :, gi] for gi in range(N_GROUPS)]
    prep_p = _rwkv_prep(pz_p, None, l, wts, rows_per_seq=S, single_step=False)
    wkv0_p = jnp.zeros((B, NH, RWKV_HEAD, RWKV_HEAD), F32)
    yr_p, wkv_p = _wkv(*prep_p, wkv0_p, l, wts, B=B, T=S)
    zprev = jnp.pad(shift0_s, ((0, Ms - Bs), (0, ZW - shift0_s.shape[1])))
    prep_s = _rwkv_prep(pz_s, zprev, l, wts, rows_per_seq=1, single_step=True)
    padded = [jnp.pad(t[:Bs].reshape(Bs, 1, -1), ((0, 0), (0, CHUNK - 1), (0, 0))).reshape(Bs * CHUNK, -1)
              for t in prep_s]
    yr_s, wkv_s = _wkv(*padded, wkv0_s, l, wts, B=Bs, T=CHUNK)
    yr_s = jnp.pad(yr_s.reshape(Bs, CHUNK, -1)[:, 0], ((0, Ms - Bs), (0, 0)))
    t1_p, t1_s = mm(att_p, att_s, wb["w_att_o"], [(pg_p, 0)], [(pg_s, 0)],
                    out_dtype=F32, epilogue=_ep_gate, name="att_out")
    mg_p, mg_s = mm(yr_p, yr_s, wb["w_rwkv_o"], [(pg_p, D), (t1_p, 0)], [(pg_s, D), (t1_s, 0)],
                    out_dtype=BF16, epilogue=_ep_gate_add, name="rwkv_out_merge")
    x1_p, x1_s = mm(mg_p, mg_s, wb["w_out"], [(xp, 0)], [(xs, 0)], out_dtype=F32, epilogue=_ep_residual,
                    name="w_out")
    hn_p = _rmsnorm(x1_p, wts["norm2_g"], l)
    hn_s = _rmsnorm(x1_s, wts["norm2_g"], l)
    hh_p, hh_s = mm(hn_p, hn_s, wb["w_ffn_up"], out_dtype=BF16, epilogue=_ep_relu2, name="ffn_up")
    tk_down = _pick(hh_p.shape[1], (2048, 1024, 512, 256, 128))
    x2_p = _matmul(hh_p, wb["w_ffn_down"], l, extras=[(x1_p, 0)], out_dtype=F32, epilogue=_ep_residual,
                   tk=tk_down, name="ffn_down")
    x2_s = _matmul(hh_s, wb["w_ffn_down"], l, extras=[(x1_s, 0)], out_dtype=F32, epilogue=_ep_residual,
                   tk=tk_down, name="ffn_down_sample")
    return (x2_p, kn_p, vn_p, pz_p, wkv_p), (x2_s, kn_s, vn_s, pz_s, wkv_s)


def _rope_tables(pos):
    half = HEAD_DIM // 2
    inv_freq = ROPE_THETA ** (-jnp.arange(half, dtype=F32) / half)
    ang = pos.astype(F32)[:, None] * inv_freq[None, :]
    cos, sin = jnp.cos(ang), jnp.sin(ang)
    return jnp.concatenate([cos, cos], axis=-1), jnp.concatenate([-sin, sin], axis=-1)


def _split_hi_lo(w):
    hi = w.astype(BF16)
    lo = (w - hi.astype(F32)).astype(BF16)
    return jnp.stack([hi, lo], axis=1)


def kernel(x_prompt, x_sample, cache_kv_w128, cache_kv_w512, cache_kv_w2048, state_wkv, state_shift,
           norm1_g, w_in, q_norm_g, k_norm_g, shift_mu, decay_w0, decay_up, iclr_a0, iclr_up, gate_up,
           key_k, key_a, bonus_rk, lnx_g, lnx_b, w_att_o, w_rwkv_o, w_out, norm2_g, w_ffn_up, w_ffn_down):
    B, S, D = x_prompt.shape
    Bs, Ts, _ = x_sample.shape
    assert Ts == 1
    DEPTH = w_in.shape[0]
    RW = decay_w0.shape[-1]
    NH = RW // RWKV_HEAD
    W = w_att_o.shape[1]
    H = W // HEAD_DIM
    att_cols = 3 * N_GROUPS * W
    rwkv_cols = 3 * RW + LORA_W + LORA_A + LORA_G
    ZW = -(-(3 * RW + LORA_W + LORA_A + LORA_G_PAD) // 1024) * 1024
    Ms = 16

    vec = lambda t: t.reshape(DEPTH, 1, -1)
    wts = {
        "norm1_g": norm1_g, "norm2_g": norm2_g, "q_norm_g": q_norm_g, "k_norm_g": k_norm_g,
        "mu_p": vec(jnp.pad(shift_mu, ((0, 0), (0, ZW - rwkv_cols)))),
        "decay_w0": vec(decay_w0), "iclr_a0": vec(iclr_a0), "key_k": vec(key_k), "key_a": vec(key_a),
        "bonus_rk": vec(bonus_rk), "lnx_g": vec(lnx_g), "lnx_b": vec(lnx_b),
        "decay_up2": _split_hi_lo(decay_up), "iclr_up2": _split_hi_lo(iclr_up),
        "gate_up2": _split_hi_lo(jnp.pad(gate_up, ((0, 0), (0, LORA_G_PAD - LORA_G), (0, 0)))),
    }
    pos_p = jnp.arange(S, dtype=jnp.int32)
    rope_p = _rope_tables(pos_p)
    caches = (cache_kv_w128, cache_kv_w512, cache_kv_w2048)

    yp = x_prompt.reshape(B * S, D)
    ys = jnp.pad(x_sample.reshape(Bs, D), ((0, Ms - Bs), (0, 0)))
    names = ("pk0", "pk1", "pk2", "pv0", "pv1", "pv2", "pwkv", "pshift",
             "sk0", "sk1", "sk2", "sv0", "sv1", "sv2", "swkv", "sshift")
    outs = {k_: [] for k_ in names}
    rope_s = _rope_tables(PAST_LEN + jnp.arange(Ts, dtype=jnp.int32))
    assert att_cols + ZW <= w_in.shape[2]
    w_in_b = w_in.astype(BF16)
    wb = {"w_in": w_in_b, "w_gate": w_in_b[:, :, att_cols + rwkv_cols:],
          "w_att_o": w_att_o, "w_rwkv_o": w_rwkv_o, "w_out": w_out,
          "w_ffn_up": w_ffn_up, "w_ffn_down": w_ffn_down}
    for l in range(DEPTH):
        (yp, kn_p, vn_p, z_p, wkv_p), (ys, kn_s, vn_s, z_s, wkv_s) = _layer(
            yp, ys, l, wb, wts, rope_p, rope_s, B=B, S=S, Bs=Bs, caches=caches,
            wkv0_s=state_wkv[l], shift0_s=state_shift[l])
        for gi in range(N_GROUPS):
            outs[f"pk{gi}"].append(kn_p[gi])
            outs[f"pv{gi}"].append(vn_p[gi])
            outs[f"sk{gi}"].append(kn_s[gi])
            outs[f"sv{gi}"].append(vn_s[gi])
        outs["pwkv"].append(wkv_p)
        outs["swkv"].append(wkv_s)
        outs["pshift"].append(z_p.reshape(B, S, ZW)[:, -1, :rwkv_cols])
        outs["sshift"].append(z_s[:Bs, :rwkv_cols])
    st = lambda n: jnp.stack(outs[n])
    new_kv_p = [jnp.stack([st(f"pk{gi}"), st(f"pv{gi}")], axis=3) for gi in range(N_GROUPS)]
    new_kv_s = [_shift_append(caches[gi], jnp.stack([st(f"sk{gi}"), st(f"sv{gi}")], axis=3))
                for gi in range(N_GROUPS)]
    return (yp.reshape(B, S, D), ys[:Bs].reshape(Bs, 1, D), *new_kv_p, st("pwkv"), st("pshift"),
            *new_kv_s, st("swkv"), st("sshift"))
```

```python
import functools
import math

import jax
import jax.numpy as jnp
from jax import lax
from jax.experimental import pallas as pl
from jax.experimental.pallas import tpu as pltpu

F32 = jnp.float32
BF16 = jnp.bfloat16

N_GROUPS = 3
ATT_GROUPS = ((128, 1), (512, 4), (2048, 16))
HEAD_DIM = 128
RWKV_HEAD = 64
LORA_W = 128
LORA_A = 128
LORA_G = 480
ROPE_THETA = 10000.0
PAST_LEN = 8192
NEG_INF = -1e30
GN_EPS = 64e-5
NORM_EPS = 1e-6

LANES = 128
VMEM_LIMIT_BYTES = 56 * 1024 * 1024

CHUNK = 64
LORA_G_PAD = 512
SPAN = 128


def _cparams(sem):
    return pltpu.CompilerParams(dimension_semantics=sem, vmem_limit_bytes=VMEM_LIMIT_BYTES)


def _pick(n, prefs):
    for p in prefs:
        if n % p == 0:
            return p
    return n


def _split(x, n):
    parts, rem = [], x
    for i in range(n):
        p = rem.astype(BF16)
        parts.append(p)
        if i + 1 < n:
            rem = rem - p.astype(F32)
    return parts


_NN = (((1,), (0,)), ((), ()))
_NT = (((1,), (1,)), ((), ()))
_TN = (((0,), (0,)), ((), ()))


def _mmp(a_parts, b_parts, dims=_NN):
    order = max(len(a_parts), len(b_parts))
    acc = None
    for i, a in enumerate(a_parts):
        for j, b in enumerate(b_parts):
            if i + j < order:
                t = lax.dot_general(a, b, dims, preferred_element_type=F32)
                acc = t if acc is None else acc + t
    return acc


def _rms_kernel(x_ref, g_ref, o_ref):
    x = x_ref[...]
    ms = jnp.mean(x * x, axis=-1, keepdims=True)
    o_ref[...] = (x * lax.rsqrt(ms + NORM_EPS) * g_ref[...]).astype(o_ref.dtype)


def _rmsnorm(x, g, l):
    M, D = x.shape
    tm = _pick(M, (256, 128, 64, 32, 16, 8))
    return pl.pallas_call(
        _rms_kernel,
        grid=(M // tm,),
        in_specs=[pl.BlockSpec((tm, D), lambda i: (i, 0)),
                  pl.BlockSpec((None, 1, D), lambda i: (l, 0, 0))],
        out_specs=pl.BlockSpec((tm, D), lambda i: (i, 0)),
        out_shape=jax.ShapeDtypeStruct((M, D), BF16),
        compiler_params=_cparams(("parallel",)),
        name="rmsnorm",
    )(x, g.reshape(g.shape[0], 1, D))


def _ep_none(acc):
    return acc


def _ep_relu2(acc):
    return jnp.square(jnp.maximum(acc, 0.0))


def _ep_residual(acc, x):
    return x + acc


def _mm_kernel(*refs, nk, n_extra, epilogue):
    a_ref, w_ref = refs[0], refs[1]
    extra = refs[2:2 + n_extra]
    o_ref = refs[2 + n_extra]
    if nk == 1:
        acc = jnp.dot(a_ref[...], w_ref[...], preferred_element_type=F32)
        o_ref[...] = epilogue(acc, *[e[...] for e in extra]).astype(o_ref.dtype)
        return
    acc_ref = refs[3 + n_extra]
    k = pl.program_id(2)

    @pl.when(k == 0)
    def _():
        acc_ref[...] = jnp.zeros_like(acc_ref)

    acc_ref[...] += jnp.dot(a_ref[...], w_ref[...], preferred_element_type=F32)

    @pl.when(k == nk - 1)
    def _():
        o_ref[...] = epilogue(acc_ref[...], *[e[...] for e in extra]).astype(o_ref.dtype)


def _matmul(a, w, l, *, out_dtype, epilogue=_ep_none, extras=(), tm=None, tn=None, tk=None, name="matmul"):
    M, K = a.shape
    N = w.shape[2]
    tm = tm or _pick(M, (1024, 512, 256, 128, 64, 32, 16))
    tn = tn or _pick(N, (1024, 512, 256, 128))
    tk = tk or _pick(K, (4096, 2048, 1024, 512, 256, 128))
    nk = K // tk
    assert N % tn == 0 and M % tm == 0
    in_specs = [pl.BlockSpec((tm, tk), lambda i, j, k: (i, k)),
                pl.BlockSpec((None, tk, tn), lambda i, j, k: (l, k, j))]
    args = [a, w]
    for arr, off in extras:
        assert off % tn == 0
        in_specs.append(pl.BlockSpec((tm, tn), lambda i, j, k, _o=off // tn: (i, j + _o)))
        args.append(arr)
    scratch = [] if nk == 1 else [pltpu.VMEM((tm, tn), F32)]
    return pl.pallas_call(
        functools.partial(_mm_kernel, nk=nk, n_extra=len(extras), epilogue=epilogue),
        grid=(M // tm, N // tn, nk),
        in_specs=in_specs,
        out_specs=pl.BlockSpec((tm, tn), lambda i, j, k: (i, j)),
        out_shape=jax.ShapeDtypeStruct((M, N), out_dtype),
        scratch_shapes=scratch,
        compiler_params=_cparams(("parallel", "parallel", "arbitrary")),
        name=name,
    )(*args)


def _merge_kernel(att_ref, wa_ref, yr_ref, wr_ref, ga_ref, gr_ref, o_ref):
    ya = jnp.dot(att_ref[...], wa_ref[...], preferred_element_type=F32)
    yr = jnp.dot(yr_ref[...], wr_ref[...], preferred_element_type=F32)
    merged = jax.nn.sigmoid(ga_ref[...]) * ya + jax.nn.sigmoid(gr_ref[...]) * yr
    o_ref[...] = merged.astype(o_ref.dtype)


def _out_merge(att, yr, w_att_o, w_rwkv_o, gates, l):
    M, Ka = att.shape
    Kr = yr.shape[1]
    D = w_att_o.shape[2]
    tm = _pick(M, (1024, 512, 256, 128, 64, 32, 16))
    tn = _pick(D, (512, 256, 128))
    return pl.pallas_call(
        _merge_kernel,
        grid=(M // tm, D // tn),
        in_specs=[pl.BlockSpec((tm, Ka), lambda i, j: (i, 0)),
                  pl.BlockSpec((None, Ka, tn), lambda i, j: (l, 0, j)),
                  pl.BlockSpec((tm, Kr), lambda i, j: (i, 0)),
                  pl.BlockSpec((None, Kr, tn), lambda i, j: (l, 0, j)),
                  pl.BlockSpec((tm, tn), lambda i, j: (i, j)),
                  pl.BlockSpec((tm, tn), lambda i, j, _o=D // tn: (i, j + _o))],
        out_specs=pl.BlockSpec((tm, tn), lambda i, j: (i, j)),
        out_shape=jax.ShapeDtypeStruct((M, D), BF16),
        compiler_params=_cparams(("parallel", "parallel")),
        name="out_merge",
    )(att, w_att_o, yr, w_rwkv_o, gates, gates)


def _mm_ws_kernel(w_ref, a_ref, a2_ref, o_ref, o2_ref):
    @pl.when(pl.program_id(1) == 0)
    def _():
        o2_ref[...] = jnp.dot(a2_ref[...], w_ref[...], preferred_element_type=F32).astype(o2_ref.dtype)

    o_ref[...] = jnp.dot(a_ref[...], w_ref[...], preferred_element_type=F32).astype(o_ref.dtype)


def _matmul_ws(a, a2, w, l, *, out_dtype, n_cols=None, w_col0=0, name="matmul"):
    M, K = a.shape
    M2 = a2.shape[0]
    N = n_cols or w.shape[2]
    tm = _pick(M, (1024, 512, 256, 128, 64, 32, 16))
    tn = _pick(math.gcd(N, w_col0) if w_col0 else N, (1024, 512, 256, 128))
    assert w_col0 % tn == 0 and N % tn == 0 and M % tm == 0
    return pl.pallas_call(
        _mm_ws_kernel,
        grid=(N // tn, M // tm),
        in_specs=[pl.BlockSpec((None, K, tn), lambda j, i, _o=w_col0 // tn: (l, 0, j + _o)),
                  pl.BlockSpec((tm, K), lambda j, i: (i, 0)),
                  pl.BlockSpec((M2, K), lambda j, i: (0, 0))],
        out_specs=[pl.BlockSpec((tm, tn), lambda j, i: (i, j)),
                   pl.BlockSpec((M2, tn), lambda j, i: (0, j))],
        out_shape=[jax.ShapeDtypeStruct((M, N), out_dtype), jax.ShapeDtypeStruct((M2, N), out_dtype)],
        compiler_params=_cparams(("parallel", "arbitrary")),
        name=name,
    )(w, a, a2)


def _norm_rope(x, g, cos, sin):
    ms = jnp.mean(x * x, axis=-1, keepdims=True)
    y = x * lax.rsqrt(ms + NORM_EPS) * g
    return y * cos + pltpu.roll(y, HEAD_DIM // 2, 1) * sin


ROW_CHUNK = 256


def _band_attn_kernel(q0, k0, v0, q1, k1, v1, q2, k2, v2, cos_ref, sin_ref, qg_ref, kg_ref,
                      att_ref, kn0, kn1, kn2, vn0, vn1, vn2, qs, ks, os_, ls, *, S):
    scale = HEAD_DIM ** -0.5
    qkv = ((q0, k0, v0, kn0, vn0), (q1, k1, v1, kn1, vn1), (q2, k2, v2, kn2, vn2))
    for gi, (q_ref, k_ref, v_ref, kn_ref, vn_ref) in enumerate(qkv):
        win, dil = ATT_GROUPS[gi]
        L = S // dil
        for c in range(S // ROW_CHUNK):
            sl = slice(c * ROW_CHUNK, (c + 1) * ROW_CHUNK)
            cos, sin = cos_ref[sl, :], sin_ref[sl, :]
            qs[sl, :] = _norm_rope(q_ref[0, sl, :], qg_ref[...], cos, sin)
            ks[sl, :] = _norm_rope(k_ref[0, sl, :], kg_ref[...], cos, sin)
        kn_ref[0] = ks[S - win:, :]
        vn_ref[0] = v_ref[0, S - win:, :]

        def rows(r, start, n, dil=dil):
            return pl.ds(start, n) if dil == 1 else pl.ds(r + start * dil, n, stride=dil)

        for r in range(dil):
            for i in range(L // SPAN):
                lo = max(i - 1, 0) * SPAN
                nkeys = (i + 1) * SPAN - lo
                qb = qs[rows(r, i * SPAN, SPAN), :].astype(BF16)
                kb = ks[rows(r, lo, nkeys), :].astype(BF16)
                vb = v_ref[0, rows(r, lo, nkeys), :].astype(BF16)
                s = lax.dot_general(qb, kb, _NT, preferred_element_type=F32) * scale
                qi = lax.broadcasted_iota(jnp.int32, (SPAN, nkeys), 0) + (i * SPAN - lo)
                ki = lax.broadcasted_iota(jnp.int32, (SPAN, nkeys), 1)
                s = jnp.where((ki <= qi) & (ki >= qi - SPAN), s, NEG_INF)
                m = jnp.max(s, axis=-1, keepdims=True)
                p = jnp.exp(s - m)
                den = jnp.sum(p, axis=-1, keepdims=True)
                o = jnp.dot(p.astype(BF16), vb, preferred_element_type=F32)
                os_[gi, rows(r, i * SPAN, SPAN), :] = o / den
                ls[gi, rows(r, i * SPAN, SPAN), :] = jnp.broadcast_to(m + jnp.log(den), (SPAN, HEAD_DIM))
    for c in range(S // ROW_CHUNK):
        sl = slice(c * ROW_CHUNK, (c + 1) * ROW_CHUNK)
        a0, a1, a2 = ls[0, sl, :], ls[1, sl, :], ls[2, sl, :]
        mx = jnp.maximum(jnp.maximum(a0, a1), a2)
        e0, e1, e2 = jnp.exp(a0 - mx), jnp.exp(a1 - mx), jnp.exp(a2 - mx)
        num = e0 * os_[0, sl, :] + e1 * os_[1, sl, :] + e2 * os_[2, sl, :]
        att_ref[0, sl, :] = (num / (e0 + e1 + e2)).astype(att_ref.dtype)


def _band_attention(proj, cos, sin, qg, kg, l, *, B, S, H):
    NP = proj.shape[1]
    G = N_GROUPS
    W = H * HEAD_DIM
    for win, dil in ATT_GROUPS:
        assert win // dil == SPAN and (S // dil) % SPAN == 0 and S % ROW_CHUNK == 0 and win <= S
    pv = proj.reshape(B, S, NP)
    qkv_specs = [pl.BlockSpec((1, S, HEAD_DIM), lambda b, h, _o=(c * G + gi) * H: (b, 0, _o + h))
                 for gi in range(G) for c in range(3)]
    tab_spec = pl.BlockSpec((S, HEAD_DIM), lambda b, h: (0, 0))
    g_spec = pl.BlockSpec((None, 1, HEAD_DIM), lambda b, h: (l, 0, 0))
    win_specs = [pl.BlockSpec((1, win, HEAD_DIM), lambda b, h: (b, 0, h)) for win, _ in ATT_GROUPS]
    win_shapes = [jax.ShapeDtypeStruct((B, win, W), F32) for win, _ in ATT_GROUPS]
    outs = pl.pallas_call(
        functools.partial(_band_attn_kernel, S=S),
        grid=(B, H),
        in_specs=qkv_specs + [tab_spec, tab_spec, g_spec, g_spec],
        out_specs=[pl.BlockSpec((1, S, HEAD_DIM), lambda b, h: (b, 0, h))] + win_specs + win_specs,
        out_shape=[jax.ShapeDtypeStruct((B, S, W), BF16)] + win_shapes + win_shapes,
        scratch_shapes=[pltpu.VMEM((S, HEAD_DIM), F32), pltpu.VMEM((S, HEAD_DIM), F32),
                        pltpu.VMEM((G, S, HEAD_DIM), F32), pltpu.VMEM((G, S, HEAD_DIM), F32)],
        compiler_params=_cparams(("parallel", "parallel")),
        name="band_attn",
    )(*([pv] * (3 * G)), cos, sin, qg.reshape(-1, 1, HEAD_DIM), kg.reshape(-1, 1, HEAD_DIM))
    att = outs[0].reshape(B * S, W)
    heads = lambda t: t.reshape(B, t.shape[1], H, HEAD_DIM)
    return att, [heads(t) for t in outs[1:1 + G]], [heads(t) for t in outs[1 + G:]]


def _cached_attn_kernel(p_ref, c0_ref, c1_ref, c2_ref, cos_ref, sin_ref, qg_ref, kg_ref,
                        att_ref, knew_ref, *, H):
    cos = cos_ref[...]
    sin = sin_ref[...]
    scale = HEAD_DIM ** -0.5
    G = N_GROUPS
    W = H * HEAD_DIM
    caches = (c0_ref, c1_ref, c2_ref)
    for h in range(H):
        ms, dens, outs = [], [], []
        for gi in range(G):
            def col(c):
                o = ((c * G + gi) * H + h) * HEAD_DIM
                return p_ref[0, :, o:o + HEAD_DIM]
            q = _norm_rope(col(0), qg_ref[...], cos, sin)
            k = _norm_rope(col(1), kg_ref[...], cos, sin)
            v = col(2)
            knew_ref[0, :, (gi * H + h) * HEAD_DIM:(gi * H + h + 1) * HEAD_DIM] = k
            kc = caches[gi][0, 0, :, 0, 0, h, :]
            vc = caches[gi][0, 0, :, 0, 1, h, :]
            q8 = jnp.broadcast_to(q, (8, HEAD_DIM)).astype(BF16)
            s_c = lax.dot_general(q8, kc.astype(BF16), _NT, preferred_element_type=F32)[0:1] * scale
            s_n = jnp.sum(q.astype(BF16).astype(F32) * k.astype(BF16).astype(F32),
                          axis=-1, keepdims=True) * scale
            m = jnp.maximum(jnp.max(s_c, axis=-1, keepdims=True), s_n)
            p_c = jnp.exp(s_c - m)
            p_n = jnp.exp(s_n - m)
            den = jnp.sum(p_c, axis=-1, keepdims=True) + p_n
            p8 = jnp.broadcast_to(p_c, (8, SPAN)).astype(BF16)
            o = jnp.dot(p8, vc.astype(BF16), preferred_element_type=F32)[0:1]
            o = o + p_n.astype(BF16).astype(F32) * v.astype(BF16).astype(F32)
            ms.append(m)
            dens.append(den)
            outs.append(o)
        mx = jnp.maximum(jnp.maximum(ms[0], ms[1]), ms[2])
        num = jnp.zeros((1, HEAD_DIM), F32)
        tot = jnp.zeros((1, 1), F32)
        for gi in range(G):
            e = jnp.exp(ms[gi] - mx)
            num = num + e * outs[gi]
            tot = tot + e * dens[gi]
        att_ref[0, :, h * HEAD_DIM:(h + 1) * HEAD_DIM] = (num / tot).astype(att_ref.dtype)


def _cached_attention(proj_s, caches, cos, sin, qg, kg, l, *, B, H):
    NP = proj_s.shape[-1]
    W = H * HEAD_DIM
    views, specs = [], []
    for gi, (win, dil) in enumerate(ATT_GROUPS):
        c = caches[gi]
        Lb = c.shape[2]
        assert Lb == win and Lb // dil == SPAN
        views.append(c.reshape(c.shape[0], B, SPAN, dil, 2, H, HEAD_DIM))
        specs.append(pl.BlockSpec((1, 1, SPAN, 1, 2, H, HEAD_DIM), lambda b: (l, b, 0, 0, 0, 0, 0)))
    tab = pl.BlockSpec((1, HEAD_DIM), lambda b: (0, 0))
    gsp = pl.BlockSpec((None, 1, HEAD_DIM), lambda b: (l, 0, 0))
    return pl.pallas_call(
        functools.partial(_cached_attn_kernel, H=H),
        grid=(B,),
        in_specs=[pl.BlockSpec((1, 1, NP), lambda b: (b, 0, 0))] + specs + [tab, tab, gsp, gsp],
        out_specs=[pl.BlockSpec((1, 1, W), lambda b: (b, 0, 0)),
                   pl.BlockSpec((1, 1, N_GROUPS * W), lambda b: (b, 0, 0))],
        out_shape=[jax.ShapeDtypeStruct((B, 1, W), F32),
                   jax.ShapeDtypeStruct((B, 1, N_GROUPS * W), F32)],
        compiler_params=_cparams(("parallel",)),
        name="cached_attn",
    )(proj_s, *views, cos, sin, qg.reshape(-1, 1, HEAD_DIM), kg.reshape(-1, 1, HEAD_DIM))


def _shift_append_kernel(c_ref, nxt_ref, new_ref, o_ref, *, R, n_tiles):
    j = pl.program_id(2)
    o_ref[0, 0, 0:R - 1] = c_ref[0, 0, 1:R]
    o_ref[0, 0, R - 1] = jnp.where(j == n_tiles - 1, new_ref[0, 0, 0], nxt_ref[0, 0, 0])


def _shift_append(cache, new_rows):
    DEPTH, B, Lb = cache.shape[:3]
    tail = cache.shape[3:]
    R = _pick(Lb, (256, 128, 64, 32, 16, 8))
    n_tiles = Lb // R
    z = (0,) * len(tail)
    return pl.pallas_call(
        functools.partial(_shift_append_kernel, R=R, n_tiles=n_tiles),
        grid=(DEPTH, B, n_tiles),
        in_specs=[pl.BlockSpec((1, 1, R) + tail, lambda l, b, j: (l, b, j) + z),
                  pl.BlockSpec((1, 1, 1) + tail, lambda l, b, j: (l, b, jnp.minimum((j + 1) * R, Lb - 1)) + z),
                  pl.BlockSpec((1, 1, 1) + tail, lambda l, b, j: (l, b, 0) + z)],
        out_specs=pl.BlockSpec((1, 1, R) + tail, lambda l, b, j: (l, b, j) + z),
        out_shape=jax.ShapeDtypeStruct(cache.shape, cache.dtype),
        compiler_params=_cparams(("parallel", "parallel", "parallel")),
        name="shift_append",
    )(cache, cache, new_rows)


def _head_sum(x, jmat):
    return _mmp(_split(x, 2), [jmat])


def _rwkv_prep_kernel(z_ref, zp_ref, mu_ref, w0_ref, a0_ref, kk_ref, ka_ref,
                      du_ref, iu_ref, gu_ref,
                      r_ref, k_ref, v_ref, ap_ref, b_ref, ld_ref, g_ref,
                      *, RW, single_step, tiles_per_seq):
    z = z_ref[...]
    if single_step:
        zp = zp_ref[...]
    else:
        first = pl.program_id(0) % tiles_per_seq == 0
        prev_row = jnp.where(first, 0.0, zp_ref[7:8, :])
        rows = lax.broadcasted_iota(jnp.int32, z.shape, 0)
        zp = jnp.where(rows == 0, prev_row, pltpu.roll(z, 1, 0))
    zs = z + mu_ref[...] * (zp - z)
    r = zs[:, 0:RW]
    k = zs[:, RW:2 * RW]
    v = zs[:, 2 * RW:3 * RW]
    o = 3 * RW
    w_lo = zs[:, o:o + LORA_W]
    a_lo = zs[:, o + LORA_W:o + LORA_W + LORA_A]
    g_lo = zs[:, o + LORA_W + LORA_A:o + LORA_W + LORA_A + LORA_G_PAD]

    def lora(x, u_ref):
        return _mmp(_split(x, 2), [u_ref[0], u_ref[1]])

    xw = w0_ref[...] + lora(jnp.tanh(w_lo), du_ref)
    ld_ref[...] = -math.exp(-0.5) * jax.nn.sigmoid(xw)
    a = jax.nn.sigmoid(a0_ref[...] + lora(a_lo, iu_ref))
    g_ref[...] = lora(jax.nn.sigmoid(g_lo), gu_ref)
    r_ref[...] = r
    v_ref[...] = v
    k_ref[...] = k * (1.0 + (a - 1.0) * ka_ref[...])
    ri = lax.broadcasted_iota(jnp.int32, (LANES, LANES), 0) // RWKV_HEAD
    ci = lax.broadcasted_iota(jnp.int32, (LANES, LANES), 1) // RWKV_HEAD
    jmat = jnp.where(ri == ci, 1.0, 0.0).astype(BF16)
    for t in range(RW // LANES):
        sl = slice(t * LANES, (t + 1) * LANES)
        kk = k[:, sl] * kk_ref[:, sl]
        ss = _head_sum(kk * kk, jmat)
        kk = kk * lax.rsqrt(jnp.maximum(ss, 1e-24))
        ap_ref[:, sl] = -kk
        b_ref[:, sl] = kk * a[:, sl]


def _rwkv_prep(proj, zprev, l, wts, *, rows_per_seq, single_step):
    M = proj.shape[0]
    RW = wts["decay_w0"].shape[-1]
    ZW = wts["mu_p"].shape[-1]
    tm = 8 if single_step else _pick(rows_per_seq, (128, 64, 32, 16, 8))
    z_spec = pl.BlockSpec((tm, ZW), lambda i: (i, 0))
    if single_step:
        zp_arr, zp_spec = zprev, pl.BlockSpec((tm, ZW), lambda i: (i, 0))
    else:
        zp_arr = proj
        zp_spec = pl.BlockSpec((8, ZW), lambda i: (jnp.maximum(i * (tm // 8) - 1, 0), 0))

    def vec(n):
        return pl.BlockSpec((None, 1, n), lambda i: (l, 0, 0))

    def up(kdim):
        return pl.BlockSpec((None, 2, kdim, RW), lambda i: (l, 0, 0, 0))

    o_spec = pl.BlockSpec((tm, RW), lambda i: (i, 0))
    outs = pl.pallas_call(
        functools.partial(_rwkv_prep_kernel, RW=RW, single_step=single_step,
                          tiles_per_seq=max(rows_per_seq // tm, 1)),
        grid=(M // tm,),
        in_specs=[z_spec, zp_spec, vec(ZW), vec(RW), vec(RW), vec(RW), vec(RW),
                  up(LORA_W), up(LORA_A), up(LORA_G_PAD)],
        out_specs=[o_spec] * 7,
        out_shape=[jax.ShapeDtypeStruct((M, RW), F32)] * 7,
        compiler_params=_cparams(("parallel",)),
        name="rwkv_prep",
    )(proj, zp_arr, wts["mu_p"], wts["decay_w0"], wts["iclr_a0"], wts["key_k"], wts["key_a"],
      wts["decay_up2"], wts["iclr_up2"], wts["gate_up2"])
    return outs


WKV_PARTS = 1
INV_BASE = 4


def _mm(a, b, dims=_NN, parts=WKV_PARTS):
    return _mmp(_split(a, parts), _split(b, parts), dims)


def _each(f, *lists):
    return [f(*xs) for xs in zip(*lists)]


def _wkv_pairs(r, k, v, ap, b, ld, wt, consts):
    C = CHUNK
    C2 = 2 * C
    tri, m2, strict, incl, eye = consts
    cat0 = lambda *xs: jnp.concatenate(xs, axis=0)
    lc = _each(lambda x: _mmp([tri], _split(x, 3)), ld)
    lt = [x[C - 1:C, :] for x in lc]

    def stack(x):
        return jnp.where(m2, jnp.concatenate([x, x], axis=0), 0.0)

    ar2 = _each(lambda ap_, r_, lc_, ld_: cat0(stack(ap_ * jnp.exp(lc_ - ld_)), stack(r_ * jnp.exp(lc_))),
                ap, r, lc, ld)
    bk2 = _each(lambda b_, k_, lc_: cat0(stack(b_ * jnp.exp(-lc_)), stack(k_ * jnp.exp(-lc_))), b, k, lc)
    bke = _each(lambda b_, k_, lc_, lt_: cat0(stack(b_ * jnp.exp(lt_ - lc_)), stack(k_ * jnp.exp(lt_ - lc_))),
                b, k, lc, lt)
    v2 = _each(stack, v)
    a4 = _each(lambda x_, y_: _mm(x_, y_, _NT), ar2, bk2)
    n_ab = [jnp.where(strict, a[0:C2, 0:C2], 0.0) for a in a4]
    a_akrk = [cat0(jnp.where(strict, a[0:C2, C2:], 0.0), jnp.where(incl, a[C2:, C2:], 0.0)) for a in a4]
    a_rb = [jnp.where(incl, a[C2:, 0:C2], 0.0) for a in a4]
    sw = _each(lambda x_, w_: _mm(x_, w_, _NT), ar2, wt)
    av = _each(_mm, a_akrk, v2)
    bi = lax.broadcasted_iota(jnp.int32, (C2, C2), 0)
    bj = lax.broadcasted_iota(jnp.int32, (C2, C2), 1)
    n0 = [jnp.where((bi // INV_BASE) == (bj // INV_BASE), n, 0.0) for n in n_ab]
    x = _each(lambda n_, q_: (eye + n_) + _mm(q_, eye + n_), n0, _each(_mm, n0, n0))
    blk = INV_BASE
    while blk < C:
        off_mask = ((bi // (2 * blk)) == (bj // (2 * blk))) & ((bi // blk) != (bj // blk))
        t_off = _each(lambda t_, n_: _mm(t_, jnp.where(off_mask, n_, 0.0)), x, n_ab)
        x = _each(lambda t_, to_: t_ + _mm(to_, t_), x, t_off)
        blk *= 2
    u2 = _each(lambda x_, s_, a_: _mm(x_, s_[0:C2] + a_[0:C2]), x, sw, av)
    y2 = _each(lambda s_, a_, rb_, u_: s_[C2:] + a_[C2:] + _mm(rb_, u_), sw, av, a_rb, u2)
    y = [t[0:C] + t[C:C2] for t in y2]
    wt_new = _each(lambda w_, lt_, u_, v_, e_: w_ * jnp.exp(lt_) + _mm(cat0(u_, v_), e_, _TN),
                   wt, lt, u2, v2, bke)
    return y, wt_new


def _wkv_kernel(r_ref, k_ref, v_ref, ap_ref, b_ref, ld_ref, g_ref, s0_ref,
                brk_ref, lg_ref, lb_ref, y_ref, sout_ref, wt_scr, *, n_pairs, n_chunks):
    C = CHUNK
    c = pl.program_id(2)
    HD = RWKV_HEAD

    @pl.when(c == 0)
    def _():
        wt_scr[...] = jnp.zeros_like(wt_scr)
        for pi in range(n_pairs):
            wt_scr[pi, 0:HD, 0:HD] = s0_ref[0, 2 * pi]
            wt_scr[pi, HD:2 * HD, HD:2 * HD] = s0_ref[0, 2 * pi + 1]

    ri = lax.broadcasted_iota(jnp.int32, (2 * C, LANES), 0)
    ci = lax.broadcasted_iota(jnp.int32, (2 * C, LANES), 1)
    tri_r = lax.broadcasted_iota(jnp.int32, (C, C), 0)
    tri_c = lax.broadcasted_iota(jnp.int32, (C, C), 1)
    tri = jnp.where(tri_c <= tri_r, 1.0, 0.0).astype(BF16)
    m2 = (ri // C) == (ci // HD)
    same = (ri // C) == (ci // C)
    strict = same & ((ci % C) < (ri % C))
    incl = same & ((ci % C) <= (ri % C))
    eye = jnp.where(ri == ci, 1.0, 0.0)
    jmat = jnp.where((ri // HD) == (ci // HD), 1.0, 0.0).astype(BF16)
    consts = (tri, m2, strict, incl, eye)
    inv = 1.0 / HD
    sls = [slice(pi * LANES, (pi + 1) * LANES) for pi in range(n_pairs)]
    col = lambda ref: [ref[0, :, sl] for sl in sls]
    r, k, v = col(r_ref), col(k_ref), col(v_ref)
    y, wt_new = _wkv_pairs(r, k, v, col(ap_ref), col(b_ref), col(ld_ref),
                           [wt_scr[pi] for pi in range(n_pairs)], consts)
    for pi in range(n_pairs):
        wt_scr[pi] = wt_new[pi]
    mu = [_head_sum(t, jmat) * inv for t in y]
    d = _each(lambda y_, m_: y_ - m_, y, mu)
    var = [_head_sum(t * t, jmat) * inv for t in d]
    rk = _each(lambda r_, k_, sl: _head_sum(r_ * k_ * brk_ref[:, sl], jmat), r, k, sls)
    for pi, sl in enumerate(sls):
        yn = d[pi] * lax.rsqrt(var[pi] + GN_EPS) * lg_ref[:, sl] + lb_ref[:, sl]
        y_ref[0, :, sl] = ((yn + rk[pi] * v[pi]) * g_ref[0, :, sl]).astype(y_ref.dtype)

    @pl.when(c == n_chunks - 1)
    def _():
        for pi in range(n_pairs):
            sout_ref[0, 2 * pi] = wt_scr[pi, 0:HD, 0:HD]
            sout_ref[0, 2 * pi + 1] = wt_scr[pi, HD:2 * HD, HD:2 * HD]


def _wkv(r, k, v, ap, b, ld, g, s0, l, wts, *, B, T):
    RW = r.shape[-1]
    NH = RW // RWKV_HEAD
    WT = _pick(RW, (1024, 512, 256, 128))
    n_pairs = WT // LANES
    n_chunks = T // CHUNK
    x_spec = pl.BlockSpec((1, CHUNK, WT), lambda bb, j, c: (bb, c, j))
    s_spec = pl.BlockSpec((1, 2 * n_pairs, RWKV_HEAD, RWKV_HEAD), lambda bb, j, c: (bb, j, 0, 0))
    p_spec = pl.BlockSpec((None, 1, WT), lambda bb, j, c: (l, 0, j))
    sh = lambda t: t.reshape(B, T, RW)
    y, s_new = pl.pallas_call(
        functools.partial(_wkv_kernel, n_pairs=n_pairs, n_chunks=n_chunks),
        grid=(B, RW // WT, n_chunks),
        in_specs=[x_spec] * 7 + [s_spec, p_spec, p_spec, p_spec],
        out_specs=[x_spec, s_spec],
        out_shape=[jax.ShapeDtypeStruct((B, T, RW), BF16),
                   jax.ShapeDtypeStruct((B, NH, RWKV_HEAD, RWKV_HEAD), F32)],
        scratch_shapes=[pltpu.VMEM((n_pairs, LANES, LANES), F32)],
        compiler_params=_cparams(("parallel", "parallel", "arbitrary")),
        name="wkv_chunk",
    )(sh(r), sh(k), sh(v), sh(ap), sh(b), sh(ld), sh(g), s0,
      wts["bonus_rk"], wts["lnx_g"], wts["lnx_b"])
    return y.reshape(B * T, RW), s_new


def _layer(xp, xs, l, wb, wts, rope_p, rope_s, *, B, S, Bs, caches, wkv0_s, shift0_s):
    D = xp.shape[1]
    Ms = xs.shape[0]
    W = wb["w_att_o"].shape[1]
    H = W // HEAD_DIM
    NH = wts["decay_w0"].shape[-1] // RWKV_HEAD
    ZW = wts["mu_p"].shape[-1]
    att_cols = 3 * N_GROUPS * W

    def mm(a_p, a_s, w, **kw):
        return _matmul_ws(a_p, a_s, w, l, **kw)

    xn_p = _rmsnorm(xp, wts["norm1_g"], l)
    xn_s = _rmsnorm(xs, wts["norm1_g"], l)
    pa_p, pa_s = mm(xn_p, xn_s, wb["w_in"], n_cols=att_cols, out_dtype=F32, name="in_proj_att")
    pz_p, pz_s = mm(xn_p, xn_s, wb["w_in"], n_cols=ZW, w_col0=att_cols, out_dtype=F32, name="in_proj_rwkv")
    pg_p, pg_s = mm(xn_p, xn_s, wb["w_gate"], out_dtype=F32, name="in_proj_gate")
    att_p, kn_p, vn_p = _band_attention(pa_p, *rope_p, wts["q_norm_g"], wts["k_norm_g"], l, B=B, S=S, H=H)
    att_s, kn = _cached_attention(pa_s[:Bs].reshape(Bs, 1, -1), caches, *rope_s,
                                  wts["q_norm_g"], wts["k_norm_g"], l, B=Bs, H=H)
    att_s = jnp.pad(att_s.reshape(Bs, W).astype(BF16), ((0, Ms - Bs), (0, 0)))
    kn = kn.reshape(Bs, 1, N_GROUPS, H, HEAD_DIM)
    kn_s = [kn[:, :, gi] for gi in range(N_GROUPS)]
    vv = pa_s[:Bs, 2 * N_GROUPS * W:].reshape(Bs, 1, N_GROUPS, H, HEAD_DIM)
    vn_s = [vv[:, :, gi] for gi in range(N_GROUPS)]
    prep_p = _rwkv_prep(pz_p, None, l, wts, rows_per_seq=S, single_step=False)
    wkv0_p = jnp.zeros((B, NH, RWKV_HEAD, RWKV_HEAD), F32)
    yr_p, wkv_p = _wkv(*prep_p, wkv0_p, l, wts, B=B, T=S)
    zprev = jnp.pad(shift0_s, ((0, Ms - Bs), (0, ZW - shift0_s.shape[1])))
    prep_s = _rwkv_prep(pz_s, zprev, l, wts, rows_per_seq=1, single_step=True)
    padded = [jnp.pad(t[:Bs].reshape(Bs, 1, -1), ((0, 0), (0, CHUNK - 1), (0, 0))).reshape(Bs * CHUNK, -1)
              for t in prep_s]
    yr_s, wkv_s = _wkv(*padded, wkv0_s, l, wts, B=Bs, T=CHUNK)
    yr_s = jnp.pad(yr_s.reshape(Bs, CHUNK, -1)[:, 0], ((0, Ms - Bs), (0, 0)))
    tn_res = _pick(D, (512, 256, 128))
    tk_down = _pick(wb["w_ffn_down"].shape[1], (2048, 1024, 512, 256, 128))

    def tail(x, att, yr, pg):
        merged = _out_merge(att, yr, wb["w_att_o"], wb["w_rwkv_o"], pg, l)
        x1 = _matmul(merged, wb["w_out"], l, out_dtype=F32, epilogue=_ep_residual, extras=[(x, 0)],
                     tn=tn_res, name="w_out")
        hn = _rmsnorm(x1, wts["norm2_g"], l)
        hh = _matmul(hn, wb["w_ffn_up"], l, out_dtype=BF16, epilogue=_ep_relu2, name="ffn_up")
        return _matmul(hh, wb["w_ffn_down"], l, out_dtype=F32, epilogue=_ep_residual, extras=[(x1, 0)],
                       tk=tk_down, name="ffn_down")

    x2_p = tail(xp, att_p, yr_p, pg_p)
    x2_s = tail(xs, att_s, yr_s, pg_s)
    return (x2_p, kn_p, vn_p, pz_p, wkv_p), (x2_s, kn_s, vn_s, pz_s, wkv_s)


def _rope_tables(pos):
    half = HEAD_DIM // 2
    inv_freq = ROPE_THETA ** (-jnp.arange(half, dtype=F32) / half)
    ang = pos.astype(F32)[:, None] * inv_freq[None, :]
    cos, sin = jnp.cos(ang), jnp.sin(ang)
    return jnp.concatenate([cos, cos], axis=-1), jnp.concatenate([-sin, sin], axis=-1)


def _split_hi_lo(w):
    hi = w.astype(BF16)
    lo = (w - hi.astype(F32)).astype(BF16)
    return jnp.stack([hi, lo], axis=1)


def kernel(x_prompt, x_sample, cache_kv_w128, cache_kv_w512, cache_kv_w2048, state_wkv, state_shift,
           norm1_g, w_in, q_norm_g, k_norm_g, shift_mu, decay_w0, decay_up, iclr_a0, iclr_up, gate_up,
           key_k, key_a, bonus_rk, lnx_g, lnx_b, w_att_o, w_rwkv_o, w_out, norm2_g, w_ffn_up, w_ffn_down):
    B, S, D = x_prompt.shape
    Bs, Ts, _ = x_sample.shape
    assert Ts == 1
    DEPTH = w_in.shape[0]
    RW = decay_w0.shape[-1]
    NH = RW // RWKV_HEAD
    W = w_att_o.shape[1]
    H = W // HEAD_DIM
    att_cols = 3 * N_GROUPS * W
    rwkv_cols = 3 * RW + LORA_W + LORA_A + LORA_G
    ZW = -(-(3 * RW + LORA_W + LORA_A + LORA_G_PAD) // 1024) * 1024
    Ms = 16

    vec = lambda t: t.reshape(DEPTH, 1, -1)
    wts = {
        "norm1_g": norm1_g, "norm2_g": norm2_g, "q_norm_g": q_norm_g, "k_norm_g": k_norm_g,
        "mu_p": vec(jnp.pad(shift_mu, ((0, 0), (0, ZW - rwkv_cols)))),
        "decay_w0": vec(decay_w0), "iclr_a0": vec(iclr_a0), "key_k": vec(key_k), "key_a": vec(key_a),
        "bonus_rk": vec(bonus_rk), "lnx_g": vec(lnx_g), "lnx_b": vec(lnx_b),
        "decay_up2": _split_hi_lo(decay_up), "iclr_up2": _split_hi_lo(iclr_up),
        "gate_up2": _split_hi_lo(jnp.pad(gate_up, ((0, 0), (0, LORA_G_PAD - LORA_G), (0, 0)))),
    }
    pos_p = jnp.arange(S, dtype=jnp.int32)
    rope_p = _rope_tables(pos_p)
    caches = (cache_kv_w128, cache_kv_w512, cache_kv_w2048)

    yp = x_prompt.reshape(B * S, D)
    ys = jnp.pad(x_sample.reshape(Bs, D), ((0, Ms - Bs), (0, 0)))
    names = ("pk0", "pk1", "pk2", "pv0", "pv1", "pv2", "pwkv", "pshift",
             "sk0", "sk1", "sk2", "sv0", "sv1", "sv2", "swkv", "sshift")
    outs = {k_: [] for k_ in names}
    rope_s = _rope_tables(PAST_LEN + jnp.arange(Ts, dtype=jnp.int32))
    assert att_cols + ZW <= w_in.shape[2]
    w_in_b = w_in.astype(BF16)
    wb = {"w_in": w_in_b, "w_gate": w_in_b[:, :, att_cols + rwkv_cols:],
          "w_att_o": w_att_o.astype(BF16), "w_rwkv_o": w_rwkv_o.astype(BF16), "w_out": w_out.astype(BF16),
          "w_ffn_up": w_ffn_up.astype(BF16), "w_ffn_down": w_ffn_down.astype(BF16)}
    for l in range(DEPTH):
        (yp, kn_p, vn_p, z_p, wkv_p), (ys, kn_s, vn_s, z_s, wkv_s) = _layer(
            yp, ys, l, wb, wts, rope_p, rope_s, B=B, S=S, Bs=Bs, caches=caches,
            wkv0_s=state_wkv[l], shift0_s=state_shift[l])
        for gi in range(N_GROUPS):
            outs[f"pk{gi}"].append(kn_p[gi])
            outs[f"pv{gi}"].append(vn_p[gi])
            outs[f"sk{gi}"].append(kn_s[gi])
            outs[f"sv{gi}"].append(vn_s[gi])
        outs["pwkv"].append(wkv_p)
        outs["swkv"].append(wkv_s)
        outs["pshift"].append(z_p.reshape(B, S, ZW)[:, -1, :rwkv_cols])
        outs["sshift"].append(z_s[:Bs, :rwkv_cols])
    st = lambda n: jnp.stack(outs[n])
    new_kv_p = [jnp.stack([st(f"pk{gi}"), st(f"pv{gi}")], axis=3) for gi in range(N_GROUPS)]
    new_kv_s = [_shift_append(caches[gi], jnp.stack([st(f"sk{gi}"), st(f"sv{gi}")], axis=3))
                for gi in range(N_GROUPS)]
    return (yp.reshape(B, S, D), ys[:Bs].reshape(Bs, 1, D), *new_kv_p, st("pwkv"), st("pshift"),
            *new_kv_s, st("swkv"), st("sshift"))
```

```python
import functools
import math

import jax
import jax.numpy as jnp
from jax import lax
from jax.experimental import pallas as pl
from jax.experimental.pallas import tpu as pltpu

F32 = jnp.float32
BF16 = jnp.bfloat16

N_GROUPS = 3
ATT_GROUPS = ((128, 1), (512, 4), (2048, 16))
HEAD_DIM = 128
RWKV_HEAD = 64
LORA_W = 128
LORA_A = 128
LORA_G = 480
ROPE_THETA = 10000.0
PAST_LEN = 8192
NEG_INF = -1e30
GN_EPS = 64e-5
NORM_EPS = 1e-6

LANES = 128
VMEM_LIMIT_BYTES = 56 * 1024 * 1024

CHUNK = 64
LORA_G_PAD = 512
SPAN = 128


def _cparams(sem):
    return pltpu.CompilerParams(dimension_semantics=sem, vmem_limit_bytes=VMEM_LIMIT_BYTES)


def _pick(n, prefs):
    for p in prefs:
        if n % p == 0:
            return p
    return n


def _split(x, n):
    parts, rem = [], x
    for i in range(n):
        p = rem.astype(BF16)
        parts.append(p)
        if i + 1 < n:
            rem = rem - p.astype(F32)
    return parts


_NN = (((1,), (0,)), ((), ()))
_NT = (((1,), (1,)), ((), ()))
_TN = (((0,), (0,)), ((), ()))


def _mmp(a_parts, b_parts, dims=_NN):
    order = max(len(a_parts), len(b_parts))
    acc = None
    for i, a in enumerate(a_parts):
        for j, b in enumerate(b_parts):
            if i + j < order:
                t = lax.dot_general(a, b, dims, preferred_element_type=F32)
                acc = t if acc is None else acc + t
    return acc


def _rms_kernel(x_ref, g_ref, o_ref):
    x = x_ref[...]
    ms = jnp.mean(x * x, axis=-1, keepdims=True)
    o_ref[...] = (x * lax.rsqrt(ms + NORM_EPS) * g_ref[...]).astype(o_ref.dtype)


def _rmsnorm(x, g, l):
    M, D = x.shape
    tm = _pick(M, (256, 128, 64, 32, 16, 8))
    return pl.pallas_call(
        _rms_kernel,
        grid=(M // tm,),
        in_specs=[pl.BlockSpec((tm, D), lambda i: (i, 0)),
                  pl.BlockSpec((None, 1, D), lambda i: (l, 0, 0))],
        out_specs=pl.BlockSpec((tm, D), lambda i: (i, 0)),
        out_shape=jax.ShapeDtypeStruct((M, D), BF16),
        compiler_params=_cparams(("parallel",)),
        name="rmsnorm",
    )(x, g.reshape(g.shape[0], 1, D))


def _ep_none(acc):
    return acc


def _ep_relu2(acc):
    return jnp.square(jnp.maximum(acc, 0.0))


def _ep_residual(acc, x):
    return x + acc


def _mm_kernel(*refs, nk, n_extra, epilogue):
    a_ref, w_ref = refs[0], refs[1]
    extra = refs[2:2 + n_extra]
    o_ref = refs[2 + n_extra]
    if nk == 1:
        acc = jnp.dot(a_ref[...], w_ref[...], preferred_element_type=F32)
        o_ref[...] = epilogue(acc, *[e[...] for e in extra]).astype(o_ref.dtype)
        return
    acc_ref = refs[3 + n_extra]
    k = pl.program_id(2)

    @pl.when(k == 0)
    def _():
        acc_ref[...] = jnp.zeros_like(acc_ref)

    acc_ref[...] += jnp.dot(a_ref[...], w_ref[...], preferred_element_type=F32)

    @pl.when(k == nk - 1)
    def _():
        o_ref[...] = epilogue(acc_ref[...], *[e[...] for e in extra]).astype(o_ref.dtype)


def _matmul(a, w, l, *, out_dtype, epilogue=_ep_none, extras=(), tm=None, tn=None, tk=None, name="matmul"):
    M, K = a.shape
    N = w.shape[2]
    tm = tm or _pick(M, (1024, 512, 256, 128, 64, 32, 16))
    tn = tn or _pick(N, (1024, 512, 256, 128))
    tk = tk or _pick(K, (4096, 2048, 1024, 512, 256, 128))
    nk = K // tk
    assert N % tn == 0 and M % tm == 0
    in_specs = [pl.BlockSpec((tm, tk), lambda i, j, k: (i, k)),
                pl.BlockSpec((None, tk, tn), lambda i, j, k: (l, k, j))]
    args = [a, w]
    for arr, off in extras:
        assert off % tn == 0
        in_specs.append(pl.BlockSpec((tm, tn), lambda i, j, k, _o=off // tn: (i, j + _o)))
        args.append(arr)
    scratch = [] if nk == 1 else [pltpu.VMEM((tm, tn), F32)]
    return pl.pallas_call(
        functools.partial(_mm_kernel, nk=nk, n_extra=len(extras), epilogue=epilogue),
        grid=(M // tm, N // tn, nk),
        in_specs=in_specs,
        out_specs=pl.BlockSpec((tm, tn), lambda i, j, k: (i, j)),
        out_shape=jax.ShapeDtypeStruct((M, N), out_dtype),
        scratch_shapes=scratch,
        compiler_params=_cparams(("parallel", "parallel", "arbitrary")),
        name=name,
    )(*args)


def _merge_kernel(att_ref, wa_ref, yr_ref, wr_ref, ga_ref, gr_ref, o_ref):
    ya = jnp.dot(att_ref[...], wa_ref[...], preferred_element_type=F32)
    yr = jnp.dot(yr_ref[...], wr_ref[...], preferred_element_type=F32)
    merged = jax.nn.sigmoid(ga_ref[...]) * ya + jax.nn.sigmoid(gr_ref[...]) * yr
    o_ref[...] = merged.astype(o_ref.dtype)


def _out_merge(att, yr, w_att_o, w_rwkv_o, gates, l):
    M, Ka = att.shape
    Kr = yr.shape[1]
    D = w_att_o.shape[2]
    tm = _pick(M, (1024, 512, 256, 128, 64, 32, 16))
    tn = _pick(D, (512, 256, 128))
    return pl.pallas_call(
        _merge_kernel,
        grid=(M // tm, D // tn),
        in_specs=[pl.BlockSpec((tm, Ka), lambda i, j: (i, 0)),
                  pl.BlockSpec((None, Ka, tn), lambda i, j: (l, 0, j)),
                  pl.BlockSpec((tm, Kr), lambda i, j: (i, 0)),
                  pl.BlockSpec((None, Kr, tn), lambda i, j: (l, 0, j)),
                  pl.BlockSpec((tm, tn), lambda i, j: (i, j)),
                  pl.BlockSpec((tm, tn), lambda i, j, _o=D // tn: (i, j + _o))],
        out_specs=pl.BlockSpec((tm, tn), lambda i, j: (i, j)),
        out_shape=jax.ShapeDtypeStruct((M, D), BF16),
        compiler_params=_cparams(("parallel", "parallel")),
        name="out_merge",
    )(att, w_att_o, yr, w_rwkv_o, gates, gates)


def _mm_ws_kernel(w_ref, a_ref, a2_ref, o_ref, o2_ref, *, epilogue):
    @pl.when(pl.program_id(1) == 0)
    def _():
        acc2 = jnp.dot(a2_ref[...], w_ref[...], preferred_element_type=F32)
        o2_ref[...] = epilogue(acc2).astype(o2_ref.dtype)

    acc = jnp.dot(a_ref[...], w_ref[...], preferred_element_type=F32)
    o_ref[...] = epilogue(acc).astype(o_ref.dtype)


def _matmul_ws(a, a2, w, l, *, out_dtype, n_cols=None, w_col0=0, epilogue=_ep_none, name="matmul"):
    M, K = a.shape
    M2 = a2.shape[0]
    N = n_cols or w.shape[2]
    tm = _pick(M, (1024, 512, 256, 128, 64, 32, 16))
    tn = _pick(math.gcd(N, w_col0) if w_col0 else N, (1024, 512, 256, 128))
    assert w_col0 % tn == 0 and N % tn == 0 and M % tm == 0
    return pl.pallas_call(
        functools.partial(_mm_ws_kernel, epilogue=epilogue),
        grid=(N // tn, M // tm),
        in_specs=[pl.BlockSpec((None, K, tn), lambda j, i, _o=w_col0 // tn: (l, 0, j + _o)),
                  pl.BlockSpec((tm, K), lambda j, i: (i, 0)),
                  pl.BlockSpec((M2, K), lambda j, i: (0, 0))],
        out_specs=[pl.BlockSpec((tm, tn), lambda j, i: (i, j)),
                   pl.BlockSpec((M2, tn), lambda j, i: (0, j))],
        out_shape=[jax.ShapeDtypeStruct((M, N), out_dtype), jax.ShapeDtypeStruct((M2, N), out_dtype)],
        compiler_params=_cparams(("parallel", "arbitrary")),
        name=name,
    )(w, a, a2)


def _norm_rope(x, g, cos, sin):
    ms = jnp.mean(x * x, axis=-1, keepdims=True)
    y = x * lax.rsqrt(ms + NORM_EPS) * g
    return y * cos + pltpu.roll(y, HEAD_DIM // 2, 1) * sin


ROW_CHUNK = 256


def _band_attn_kernel(q0, k0, v0, q1, k1, v1, q2, k2, v2, cos_ref, sin_ref, qg_ref, kg_ref,
                      att_ref, kn0, kn1, kn2, vn0, vn1, vn2, qs, ks, os_, ls, *, S):
    scale = HEAD_DIM ** -0.5
    qkv = ((q0, k0, v0, kn0, vn0), (q1, k1, v1, kn1, vn1), (q2, k2, v2, kn2, vn2))
    for gi, (q_ref, k_ref, v_ref, kn_ref, vn_ref) in enumerate(qkv):
        win, dil = ATT_GROUPS[gi]
        L = S // dil
        for c in range(S // ROW_CHUNK):
            sl = slice(c * ROW_CHUNK, (c + 1) * ROW_CHUNK)
            cos, sin = cos_ref[sl, :], sin_ref[sl, :]
            qs[sl, :] = _norm_rope(q_ref[0, sl, :], qg_ref[...], cos, sin)
            ks[sl, :] = _norm_rope(k_ref[0, sl, :], kg_ref[...], cos, sin)
        kn_ref[0] = ks[S - win:, :]
        vn_ref[0] = v_ref[0, S - win:, :]

        def rows(r, start, n, dil=dil):
            return pl.ds(start, n) if dil == 1 else pl.ds(r + start * dil, n, stride=dil)

        for r in range(dil):
            for i in range(L // SPAN):
                lo = max(i - 1, 0) * SPAN
                nkeys = (i + 1) * SPAN - lo
                qb = qs[rows(r, i * SPAN, SPAN), :].astype(BF16)
                kb = ks[rows(r, lo, nkeys), :].astype(BF16)
                vb = v_ref[0, rows(r, lo, nkeys), :].astype(BF16)
                s = lax.dot_general(qb, kb, _NT, preferred_element_type=F32) * scale
                qi = lax.broadcasted_iota(jnp.int32, (SPAN, nkeys), 0) + (i * SPAN - lo)
                ki = lax.broadcasted_iota(jnp.int32, (SPAN, nkeys), 1)
                s = jnp.where((ki <= qi) & (ki >= qi - SPAN), s, NEG_INF)
                m = jnp.max(s, axis=-1, keepdims=True)
                p = jnp.exp(s - m)
                den = jnp.sum(p, axis=-1, keepdims=True)
                o = jnp.dot(p.astype(BF16), vb, preferred_element_type=F32)
                os_[gi, rows(r, i * SPAN, SPAN), :] = o / den
                ls[gi, rows(r, i * SPAN, SPAN), :] = jnp.broadcast_to(m + jnp.log(den), (SPAN, HEAD_DIM))
    for c in range(S // ROW_CHUNK):
        sl = slice(c * ROW_CHUNK, (c + 1) * ROW_CHUNK)
        a0, a1, a2 = ls[0, sl, :], ls[1, sl, :], ls[2, sl, :]
        mx = jnp.maximum(jnp.maximum(a0, a1), a2)
        e0, e1, e2 = jnp.exp(a0 - mx), jnp.exp(a1 - mx), jnp.exp(a2 - mx)
        num = e0 * os_[0, sl, :] + e1 * os_[1, sl, :] + e2 * os_[2, sl, :]
        att_ref[0, sl, :] = (num / (e0 + e1 + e2)).astype(att_ref.dtype)


def _band_attention(proj, cos, sin, qg, kg, l, *, B, S, H):
    NP = proj.shape[1]
    G = N_GROUPS
    W = H * HEAD_DIM
    for win, dil in ATT_GROUPS:
        assert win // dil == SPAN and (S // dil) % SPAN == 0 and S % ROW_CHUNK == 0 and win <= S
    pv = proj.reshape(B, S, NP)
    qkv_specs = [pl.BlockSpec((1, S, HEAD_DIM), lambda b, h, _o=(c * G + gi) * H: (b, 0, _o + h))
                 for gi in range(G) for c in range(3)]
    tab_spec = pl.BlockSpec((S, HEAD_DIM), lambda b, h: (0, 0))
    g_spec = pl.BlockSpec((None, 1, HEAD_DIM), lambda b, h: (l, 0, 0))
    win_specs = [pl.BlockSpec((1, win, HEAD_DIM), lambda b, h: (b, 0, h)) for win, _ in ATT_GROUPS]
    win_shapes = [jax.ShapeDtypeStruct((B, win, W), F32) for win, _ in ATT_GROUPS]
    outs = pl.pallas_call(
        functools.partial(_band_attn_kernel, S=S),
        grid=(B, H),
        in_specs=qkv_specs + [tab_spec, tab_spec, g_spec, g_spec],
        out_specs=[pl.BlockSpec((1, S, HEAD_DIM), lambda b, h: (b, 0, h))] + win_specs + win_specs,
        out_shape=[jax.ShapeDtypeStruct((B, S, W), BF16)] + win_shapes + win_shapes,
        scratch_shapes=[pltpu.VMEM((S, HEAD_DIM), F32), pltpu.VMEM((S, HEAD_DIM), F32),
                        pltpu.VMEM((G, S, HEAD_DIM), F32), pltpu.VMEM((G, S, HEAD_DIM), F32)],
        compiler_params=_cparams(("parallel", "parallel")),
        name="band_attn",
    )(*([pv] * (3 * G)), cos, sin, qg.reshape(-1, 1, HEAD_DIM), kg.reshape(-1, 1, HEAD_DIM))
    att = outs[0].reshape(B * S, W)
    heads = lambda t: t.reshape(B, t.shape[1], H, HEAD_DIM)
    return att, [heads(t) for t in outs[1:1 + G]], [heads(t) for t in outs[1 + G:]]


def _cached_attn_kernel(p_ref, c0_ref, c1_ref, c2_ref, cos_ref, sin_ref, qg_ref, kg_ref,
                        att_ref, knew_ref, *, H):
    cos = cos_ref[...]
    sin = sin_ref[...]
    scale = HEAD_DIM ** -0.5
    G = N_GROUPS
    W = H * HEAD_DIM
    caches = (c0_ref, c1_ref, c2_ref)
    for h in range(H):
        ms, dens, outs = [], [], []
        for gi in range(G):
            def col(c):
                o = ((c * G + gi) * H + h) * HEAD_DIM
                return p_ref[0, :, o:o + HEAD_DIM]
            q = _norm_rope(col(0), qg_ref[...], cos, sin)
            k = _norm_rope(col(1), kg_ref[...], cos, sin)
            v = col(2)
            knew_ref[0, :, (gi * H + h) * HEAD_DIM:(gi * H + h + 1) * HEAD_DIM] = k
            kc = caches[gi][0, 0, :, 0, 0, h, :]
            vc = caches[gi][0, 0, :, 0, 1, h, :]
            q8 = jnp.broadcast_to(q, (8, HEAD_DIM)).astype(BF16)
            s_c = lax.dot_general(q8, kc.astype(BF16), _NT, preferred_element_type=F32)[0:1] * scale
            s_n = jnp.sum(q.astype(BF16).astype(F32) * k.astype(BF16).astype(F32),
                          axis=-1, keepdims=True) * scale
            m = jnp.maximum(jnp.max(s_c, axis=-1, keepdims=True), s_n)
            p_c = jnp.exp(s_c - m)
            p_n = jnp.exp(s_n - m)
            den = jnp.sum(p_c, axis=-1, keepdims=True) + p_n
            p8 = jnp.broadcast_to(p_c, (8, SPAN)).astype(BF16)
            o = jnp.dot(p8, vc.astype(BF16), preferred_element_type=F32)[0:1]
            o = o + p_n.astype(BF16).astype(F32) * v.astype(BF16).astype(F32)
            ms.append(m)
            dens.append(den)
            outs.append(o)
        mx = jnp.maximum(jnp.maximum(ms[0], ms[1]), ms[2])
        num = jnp.zeros((1, HEAD_DIM), F32)
        tot = jnp.zeros((1, 1), F32)
        for gi in range(G):
            e = jnp.exp(ms[gi] - mx)
            num = num + e * outs[gi]
            tot = tot + e * dens[gi]
        att_ref[0, :, h * HEAD_DIM:(h + 1) * HEAD_DIM] = (num / tot).astype(att_ref.dtype)


def _cached_attention(proj_s, caches, cos, sin, qg, kg, l, *, B, H):
    NP = proj_s.shape[-1]
    W = H * HEAD_DIM
    views, specs = [], []
    for gi, (win, dil) in enumerate(ATT_GROUPS):
        c = caches[gi]
        Lb = c.shape[2]
        assert Lb == win and Lb // dil == SPAN
        views.append(c.reshape(c.shape[0], B, SPAN, dil, 2, H, HEAD_DIM))
        specs.append(pl.BlockSpec((1, 1, SPAN, 1, 2, H, HEAD_DIM), lambda b: (l, b, 0, 0, 0, 0, 0)))
    tab = pl.BlockSpec((1, HEAD_DIM), lambda b: (0, 0))
    gsp = pl.BlockSpec((None, 1, HEAD_DIM), lambda b: (l, 0, 0))
    return pl.pallas_call(
        functools.partial(_cached_attn_kernel, H=H),
        grid=(B,),
        in_specs=[pl.BlockSpec((1, 1, NP), lambda b: (b, 0, 0))] + specs + [tab, tab, gsp, gsp],
        out_specs=[pl.BlockSpec((1, 1, W), lambda b: (b, 0, 0)),
                   pl.BlockSpec((1, 1, N_GROUPS * W), lambda b: (b, 0, 0))],
        out_shape=[jax.ShapeDtypeStruct((B, 1, W), F32),
                   jax.ShapeDtypeStruct((B, 1, N_GROUPS * W), F32)],
        compiler_params=_cparams(("parallel",)),
        name="cached_attn",
    )(proj_s, *views, cos, sin, qg.reshape(-1, 1, HEAD_DIM), kg.reshape(-1, 1, HEAD_DIM))


def _shift_append_kernel(c_ref, nxt_ref, new_ref, o_ref, *, R, n_tiles):
    j = pl.program_id(2)
    o_ref[0, 0, 0:R - 1] = c_ref[0, 0, 1:R]
    o_ref[0, 0, R - 1] = jnp.where(j == n_tiles - 1, new_ref[0, 0, 0], nxt_ref[0, 0, 0])


def _shift_append(cache, new_rows):
    DEPTH, B, Lb = cache.shape[:3]
    tail = cache.shape[3:]
    R = _pick(Lb, (256, 128, 64, 32, 16, 8))
    n_tiles = Lb // R
    z = (0,) * len(tail)
    return pl.pallas_call(
        functools.partial(_shift_append_kernel, R=R, n_tiles=n_tiles),
        grid=(DEPTH, B, n_tiles),
        in_specs=[pl.BlockSpec((1, 1, R) + tail, lambda l, b, j: (l, b, j) + z),
                  pl.BlockSpec((1, 1, 1) + tail, lambda l, b, j: (l, b, jnp.minimum((j + 1) * R, Lb - 1)) + z),
                  pl.BlockSpec((1, 1, 1) + tail, lambda l, b, j: (l, b, 0) + z)],
        out_specs=pl.BlockSpec((1, 1, R) + tail, lambda l, b, j: (l, b, j) + z),
        out_shape=jax.ShapeDtypeStruct(cache.shape, cache.dtype),
        compiler_params=_cparams(("parallel", "parallel", "parallel")),
        name="shift_append",
    )(cache, cache, new_rows)


def _head_sum(x, jmat):
    return _mmp(_split(x, 2), [jmat])


def _rwkv_prep_kernel(z_ref, zp_ref, mu_ref, w0_ref, a0_ref, kk_ref, ka_ref,
                      du_ref, iu_ref, gu_ref,
                      r_ref, k_ref, v_ref, ap_ref, b_ref, ld_ref, g_ref,
                      *, RW, single_step, tiles_per_seq):
    z = z_ref[...]
    if single_step:
        zp = zp_ref[...]
    else:
        first = pl.program_id(0) % tiles_per_seq == 0
        prev_row = jnp.where(first, 0.0, zp_ref[7:8, :])
        rows = lax.broadcasted_iota(jnp.int32, z.shape, 0)
        zp = jnp.where(rows == 0, prev_row, pltpu.roll(z, 1, 0))
    zs = z + mu_ref[...] * (zp - z)
    r = zs[:, 0:RW]
    k = zs[:, RW:2 * RW]
    v = zs[:, 2 * RW:3 * RW]
    o = 3 * RW
    w_lo = zs[:, o:o + LORA_W]
    a_lo = zs[:, o + LORA_W:o + LORA_W + LORA_A]
    g_lo = zs[:, o + LORA_W + LORA_A:o + LORA_W + LORA_A + LORA_G_PAD]

    def lora(x, u_ref):
        return _mmp(_split(x, 2), [u_ref[0], u_ref[1]])

    xw = w0_ref[...] + lora(jnp.tanh(w_lo), du_ref)
    ld_ref[...] = -math.exp(-0.5) * jax.nn.sigmoid(xw)
    a = jax.nn.sigmoid(a0_ref[...] + lora(a_lo, iu_ref))
    g_ref[...] = lora(jax.nn.sigmoid(g_lo), gu_ref)
    r_ref[...] = r
    v_ref[...] = v
    k_ref[...] = k * (1.0 + (a - 1.0) * ka_ref[...])
    ri = lax.broadcasted_iota(jnp.int32, (LANES, LANES), 0) // RWKV_HEAD
    ci = lax.broadcasted_iota(jnp.int32, (LANES, LANES), 1) // RWKV_HEAD
    jmat = jnp.where(ri == ci, 1.0, 0.0).astype(BF16)
    for t in range(RW // LANES):
        sl = slice(t * LANES, (t + 1) * LANES)
        kk = k[:, sl] * kk_ref[:, sl]
        ss = _head_sum(kk * kk, jmat)
        kk = kk * lax.rsqrt(jnp.maximum(ss, 1e-24))
        ap_ref[:, sl] = -kk
        b_ref[:, sl] = kk * a[:, sl]


def _rwkv_prep(proj, zprev, l, wts, *, rows_per_seq, single_step):
    M = proj.shape[0]
    RW = wts["decay_w0"].shape[-1]
    ZW = wts["mu_p"].shape[-1]
    tm = 8 if single_step else _pick(rows_per_seq, (128, 64, 32, 16, 8))
    z_spec = pl.BlockSpec((tm, ZW), lambda i: (i, 0))
    if single_step:
        zp_arr, zp_spec = zprev, pl.BlockSpec((tm, ZW), lambda i: (i, 0))
    else:
        zp_arr = proj
        zp_spec = pl.BlockSpec((8, ZW), lambda i: (jnp.maximum(i * (tm // 8) - 1, 0), 0))

    def vec(n):
        return pl.BlockSpec((None, 1, n), lambda i: (l, 0, 0))

    def up(kdim):
        return pl.BlockSpec((None, 2, kdim, RW), lambda i: (l, 0, 0, 0))

    o_spec = pl.BlockSpec((tm, RW), lambda i: (i, 0))
    outs = pl.pallas_call(
        functools.partial(_rwkv_prep_kernel, RW=RW, single_step=single_step,
                          tiles_per_seq=max(rows_per_seq // tm, 1)),
        grid=(M // tm,),
        in_specs=[z_spec, zp_spec, vec(ZW), vec(RW), vec(RW), vec(RW), vec(RW),
                  up(LORA_W), up(LORA_A), up(LORA_G_PAD)],
        out_specs=[o_spec] * 7,
        out_shape=[jax.ShapeDtypeStruct((M, RW), F32)] * 7,
        compiler_params=_cparams(("parallel",)),
        name="rwkv_prep",
    )(proj, zp_arr, wts["mu_p"], wts["decay_w0"], wts["iclr_a0"], wts["key_k"], wts["key_a"],
      wts["decay_up2"], wts["iclr_up2"], wts["gate_up2"])
    return outs


WKV_PARTS = 1
INV_BASE = 4


def _mm(a, b, dims=_NN, parts=WKV_PARTS):
    return _mmp(_split(a, parts), _split(b, parts), dims)


def _each(f, *lists):
    return [f(*xs) for xs in zip(*lists)]


def _wkv_pairs(r, k, v, ap, b, ld, wt, consts):
    C = CHUNK
    C2 = 2 * C
    tri, m2, strict, incl, eye = consts
    cat0 = lambda *xs: jnp.concatenate(xs, axis=0)
    lc = _each(lambda x: _mmp([tri], _split(x, 3)), ld)
    lt = [x[C - 1:C, :] for x in lc]

    def stack(x):
        return jnp.where(m2, jnp.concatenate([x, x], axis=0), 0.0)

    ar2 = _each(lambda ap_, r_, lc_, ld_: cat0(stack(ap_ * jnp.exp(lc_ - ld_)), stack(r_ * jnp.exp(lc_))),
                ap, r, lc, ld)
    bk2 = _each(lambda b_, k_, lc_: cat0(stack(b_ * jnp.exp(-lc_)), stack(k_ * jnp.exp(-lc_))), b, k, lc)
    bke = _each(lambda b_, k_, lc_, lt_: cat0(stack(b_ * jnp.exp(lt_ - lc_)), stack(k_ * jnp.exp(lt_ - lc_))),
                b, k, lc, lt)
    v2 = _each(stack, v)
    a4 = _each(lambda x_, y_: _mm(x_, y_, _NT), ar2, bk2)
    n_ab = [jnp.where(strict, a[0:C2, 0:C2], 0.0) for a in a4]
    a_akrk = [cat0(jnp.where(strict, a[0:C2, C2:], 0.0), jnp.where(incl, a[C2:, C2:], 0.0)) for a in a4]
    a_rb = [jnp.where(incl, a[C2:, 0:C2], 0.0) for a in a4]
    sw = _each(lambda x_, w_: _mm(x_, w_, _NT), ar2, wt)
    av = _each(_mm, a_akrk, v2)
    bi = lax.broadcasted_iota(jnp.int32, (C2, C2), 0)
    bj = lax.broadcasted_iota(jnp.int32, (C2, C2), 1)
    n0 = [jnp.where((bi // INV_BASE) == (bj // INV_BASE), n, 0.0) for n in n_ab]
    x = _each(lambda n_, q_: (eye + n_) + _mm(q_, eye + n_), n0, _each(_mm, n0, n0))
    blk = INV_BASE
    while blk < C:
        off_mask = ((bi // (2 * blk)) == (bj // (2 * blk))) & ((bi // blk) != (bj // blk))
        t_off = _each(lambda t_, n_: _mm(t_, jnp.where(off_mask, n_, 0.0)), x, n_ab)
        x = _each(lambda t_, to_: t_ + _mm(to_, t_), x, t_off)
        blk *= 2
    u2 = _each(lambda x_, s_, a_: _mm(x_, s_[0:C2] + a_[0:C2]), x, sw, av)
    y2 = _each(lambda s_, a_, rb_, u_: s_[C2:] + a_[C2:] + _mm(rb_, u_), sw, av, a_rb, u2)
    y = [t[0:C] + t[C:C2] for t in y2]
    wt_new = _each(lambda w_, lt_, u_, v_, e_: w_ * jnp.exp(lt_) + _mm(cat0(u_, v_), e_, _TN),
                   wt, lt, u2, v2, bke)
    return y, wt_new


def _wkv_kernel(r_ref, k_ref, v_ref, ap_ref, b_ref, ld_ref, g_ref, s0_ref,
                brk_ref, lg_ref, lb_ref, y_ref, sout_ref, wt_scr, *, n_pairs, n_chunks):
    C = CHUNK
    c = pl.program_id(2)
    HD = RWKV_HEAD

    @pl.when(c == 0)
    def _():
        wt_scr[...] = jnp.zeros_like(wt_scr)
        for pi in range(n_pairs):
            wt_scr[pi, 0:HD, 0:HD] = s0_ref[0, 2 * pi]
            wt_scr[pi, HD:2 * HD, HD:2 * HD] = s0_ref[0, 2 * pi + 1]

    ri = lax.broadcasted_iota(jnp.int32, (2 * C, LANES), 0)
    ci = lax.broadcasted_iota(jnp.int32, (2 * C, LANES), 1)
    tri_r = lax.broadcasted_iota(jnp.int32, (C, C), 0)
    tri_c = lax.broadcasted_iota(jnp.int32, (C, C), 1)
    tri = jnp.where(tri_c <= tri_r, 1.0, 0.0).astype(BF16)
    m2 = (ri // C) == (ci // HD)
    same = (ri // C) == (ci // C)
    strict = same & ((ci % C) < (ri % C))
    incl = same & ((ci % C) <= (ri % C))
    eye = jnp.where(ri == ci, 1.0, 0.0)
    jmat = jnp.where((ri // HD) == (ci // HD), 1.0, 0.0).astype(BF16)
    consts = (tri, m2, strict, incl, eye)
    inv = 1.0 / HD
    sls = [slice(pi * LANES, (pi + 1) * LANES) for pi in range(n_pairs)]
    col = lambda ref: [ref[0, :, sl] for sl in sls]
    r, k, v = col(r_ref), col(k_ref), col(v_ref)
    y, wt_new = _wkv_pairs(r, k, v, col(ap_ref), col(b_ref), col(ld_ref),
                           [wt_scr[pi] for pi in range(n_pairs)], consts)
    for pi in range(n_pairs):
        wt_scr[pi] = wt_new[pi]
    mu = [_head_sum(t, jmat) * inv for t in y]
    d = _each(lambda y_, m_: y_ - m_, y, mu)
    var = [_head_sum(t * t, jmat) * inv for t in d]
    rk = _each(lambda r_, k_, sl: _head_sum(r_ * k_ * brk_ref[:, sl], jmat), r, k, sls)
    for pi, sl in enumerate(sls):
        yn = d[pi] * lax.rsqrt(var[pi] + GN_EPS) * lg_ref[:, sl] + lb_ref[:, sl]
        y_ref[0, :, sl] = ((yn + rk[pi] * v[pi]) * g_ref[0, :, sl]).astype(y_ref.dtype)

    @pl.when(c == n_chunks - 1)
    def _():
        for pi in range(n_pairs):
            sout_ref[0, 2 * pi] = wt_scr[pi, 0:HD, 0:HD]
            sout_ref[0, 2 * pi + 1] = wt_scr[pi, HD:2 * HD, HD:2 * HD]


def _wkv(r, k, v, ap, b, ld, g, s0, l, wts, *, B, T):
    RW = r.shape[-1]
    NH = RW // RWKV_HEAD
    WT = _pick(RW, (1024, 512, 256, 128))
    n_pairs = WT // LANES
    n_chunks = T // CHUNK
    x_spec = pl.BlockSpec((1, CHUNK, WT), lambda bb, j, c: (bb, c, j))
    s_spec = pl.BlockSpec((1, 2 * n_pairs, RWKV_HEAD, RWKV_HEAD), lambda bb, j, c: (bb, j, 0, 0))
    p_spec = pl.BlockSpec((None, 1, WT), lambda bb, j, c: (l, 0, j))
    sh = lambda t: t.reshape(B, T, RW)
    y, s_new = pl.pallas_call(
        functools.partial(_wkv_kernel, n_pairs=n_pairs, n_chunks=n_chunks),
        grid=(B, RW // WT, n_chunks),
        in_specs=[x_spec] * 7 + [s_spec, p_spec, p_spec, p_spec],
        out_specs=[x_spec, s_spec],
        out_shape=[jax.ShapeDtypeStruct((B, T, RW), BF16),
                   jax.ShapeDtypeStruct((B, NH, RWKV_HEAD, RWKV_HEAD), F32)],
        scratch_shapes=[pltpu.VMEM((n_pairs, LANES, LANES), F32)],
        compiler_params=_cparams(("parallel", "parallel", "arbitrary")),
        name="wkv_chunk",
    )(sh(r), sh(k), sh(v), sh(ap), sh(b), sh(ld), sh(g), s0,
      wts["bonus_rk"], wts["lnx_g"], wts["lnx_b"])
    return y.reshape(B * T, RW), s_new


def _layer(xp, xs, l, wb, wts, rope_p, rope_s, *, B, S, Bs, caches, wkv0_s, shift0_s):
    D = xp.shape[1]
    Ms = xs.shape[0]
    W = wb["w_att_o"].shape[1]
    H = W // HEAD_DIM
    NH = wts["decay_w0"].shape[-1] // RWKV_HEAD
    ZW = wts["mu_p"].shape[-1]
    att_cols = 3 * N_GROUPS * W

    def mm(a_p, a_s, w, **kw):
        return _matmul_ws(a_p, a_s, w, l, **kw)

    xn_p = _rmsnorm(xp, wts["norm1_g"], l)
    xn_s = _rmsnorm(xs, wts["norm1_g"], l)
    pa_p, pa_s = mm(xn_p, xn_s, wb["w_in"], n_cols=att_cols, out_dtype=F32, name="in_proj_att")
    pz_p, pz_s = mm(xn_p, xn_s, wb["w_in"], n_cols=ZW, w_col0=att_cols, out_dtype=F32, name="in_proj_rwkv")
    pg_p, pg_s = mm(xn_p, xn_s, wb["w_gate"], out_dtype=F32, name="in_proj_gate")
    att_p, kn_p, vn_p = _band_attention(pa_p, *rope_p, wts["q_norm_g"], wts["k_norm_g"], l, B=B, S=S, H=H)
    att_s, kn = _cached_attention(pa_s[:Bs].reshape(Bs, 1, -1), caches, *rope_s,
                                  wts["q_norm_g"], wts["k_norm_g"], l, B=Bs, H=H)
    att_s = jnp.pad(att_s.reshape(Bs, W).astype(BF16), ((0, Ms - Bs), (0, 0)))
    kn = kn.reshape(Bs, 1, N_GROUPS, H, HEAD_DIM)
    kn_s = [kn[:, :, gi] for gi in range(N_GROUPS)]
    vv = pa_s[:Bs, 2 * N_GROUPS * W:].reshape(Bs, 1, N_GROUPS, H, HEAD_DIM)
    vn_s = [vv[:, :, gi] for gi in range(N_GROUPS)]
    prep_p = _rwkv_prep(pz_p, None, l, wts, rows_per_seq=S, single_step=False)
    wkv0_p = jnp.zeros((B, NH, RWKV_HEAD, RWKV_HEAD), F32)
    yr_p, wkv_p = _wkv(*prep_p, wkv0_p, l, wts, B=B, T=S)
    zprev = jnp.pad(shift0_s, ((0, Ms - Bs), (0, ZW - shift0_s.shape[1])))
    prep_s = _rwkv_prep(pz_s, zprev, l, wts, rows_per_seq=1, single_step=True)
    padded = [jnp.pad(t[:Bs].reshape(Bs, 1, -1), ((0, 0), (0, CHUNK - 1), (0, 0))).reshape(Bs * CHUNK, -1)
              for t in prep_s]
    yr_s, wkv_s = _wkv(*padded, wkv0_s, l, wts, B=Bs, T=CHUNK)
    yr_s = jnp.pad(yr_s.reshape(Bs, CHUNK, -1)[:, 0], ((0, Ms - Bs), (0, 0)))
    tn_res = _pick(D, (512, 256, 128))
    tk_down = _pick(wb["w_ffn_down"].shape[1], (2048, 1024, 512, 256, 128))

    def mix(x, att, yr, pg):
        merged = _out_merge(att, yr, wb["w_att_o"], wb["w_rwkv_o"], pg, l)
        x1 = _matmul(merged, wb["w_out"], l, out_dtype=F32, epilogue=_ep_residual, extras=[(x, 0)],
                     tn=tn_res, name="w_out")
        return x1, _rmsnorm(x1, wts["norm2_g"], l)

    x1_p, hn_p = mix(xp, att_p, yr_p, pg_p)
    x1_s, hn_s = mix(xs, att_s, yr_s, pg_s)
    hh_p, hh_s = mm(hn_p, hn_s, wb["w_ffn_up"], out_dtype=BF16, epilogue=_ep_relu2, name="ffn_up")
    x2_p, x2_s = [_matmul(hh, wb["w_ffn_down"], l, out_dtype=F32, epilogue=_ep_residual, extras=[(x1, 0)],
                          tk=tk_down, name="ffn_down") for hh, x1 in ((hh_p, x1_p), (hh_s, x1_s))]
    return (x2_p, kn_p, vn_p, pz_p, wkv_p), (x2_s, kn_s, vn_s, pz_s, wkv_s)


def _rope_tables(pos):
    half = HEAD_DIM // 2
    inv_freq = ROPE_THETA ** (-jnp.arange(half, dtype=F32) / half)
    ang = pos.astype(F32)[:, None] * inv_freq[None, :]
    cos, sin = jnp.cos(ang), jnp.sin(ang)
    return jnp.concatenate([cos, cos], axis=-1), jnp.concatenate([-sin, sin], axis=-1)


def _split_hi_lo(w):
    hi = w.astype(BF16)
    lo = (w - hi.astype(F32)).astype(BF16)
    return jnp.stack([hi, lo], axis=1)


def kernel(x_prompt, x_sample, cache_kv_w128, cache_kv_w512, cache_kv_w2048, state_wkv, state_shift,
           norm1_g, w_in, q_norm_g, k_norm_g, shift_mu, decay_w0, decay_up, iclr_a0, iclr_up, gate_up,
           key_k, key_a, bonus_rk, lnx_g, lnx_b, w_att_o, w_rwkv_o, w_out, norm2_g, w_ffn_up, w_ffn_down):
    B, S, D = x_prompt.shape
    Bs, Ts, _ = x_sample.shape
    assert Ts == 1
    DEPTH = w_in.shape[0]
    RW = decay_w0.shape[-1]
    NH = RW // RWKV_HEAD
    W = w_att_o.shape[1]
    H = W // HEAD_DIM
    att_cols = 3 * N_GROUPS * W
    rwkv_cols = 3 * RW + LORA_W + LORA_A + LORA_G
    ZW = -(-(3 * RW + LORA_W + LORA_A + LORA_G_PAD) // 1024) * 1024
    Ms = 16

    vec = lambda t: t.reshape(DEPTH, 1, -1)
    wts = {
        "norm1_g": norm1_g, "norm2_g": norm2_g, "q_norm_g": q_norm_g, "k_norm_g": k_norm_g,
        "mu_p": vec(jnp.pad(shift_mu, ((0, 0), (0, ZW - rwkv_cols)))),
        "decay_w0": vec(decay_w0), "iclr_a0": vec(iclr_a0), "key_k": vec(key_k), "key_a": vec(key_a),
        "bonus_rk": vec(bonus_rk), "lnx_g": vec(lnx_g), "lnx_b": vec(lnx_b),
        "decay_up2": _split_hi_lo(decay_up), "iclr_up2": _split_hi_lo(iclr_up),
        "gate_up2": _split_hi_lo(jnp.pad(gate_up, ((0, 0), (0, LORA_G_PAD - LORA_G), (0, 0)))),
    }
    pos_p = jnp.arange(S, dtype=jnp.int32)
    rope_p = _rope_tables(pos_p)
    caches = (cache_kv_w128, cache_kv_w512, cache_kv_w2048)

    yp = x_prompt.reshape(B * S, D)
    ys = jnp.pad(x_sample.reshape(Bs, D), ((0, Ms - Bs), (0, 0)))
    names = ("pk0", "pk1", "pk2", "pv0", "pv1", "pv2", "pwkv", "pshift",
             "sk0", "sk1", "sk2", "sv0", "sv1", "sv2", "swkv", "sshift")
    outs = {k_: [] for k_ in names}
    rope_s = _rope_tables(PAST_LEN + jnp.arange(Ts, dtype=jnp.int32))
    assert att_cols + ZW <= w_in.shape[2]
    w_in_b = w_in.astype(BF16)
    wb = {"w_in": w_in_b, "w_gate": w_in_b[:, :, att_cols + rwkv_cols:],
          "w_att_o": w_att_o.astype(BF16), "w_rwkv_o": w_rwkv_o.astype(BF16), "w_out": w_out.astype(BF16),
          "w_ffn_up": w_ffn_up.astype(BF16), "w_ffn_down": w_ffn_down.astype(BF16)}
    for l in range(DEPTH):
        (yp, kn_p, vn_p, z_p, wkv_p), (ys, kn_s, vn_s, z_s, wkv_s) = _layer(
            yp, ys, l, wb, wts, rope_p, rope_s, B=B, S=S, Bs=Bs, caches=caches,
            wkv0_s=state_wkv[l], shift0_s=state_shift[l])
        for gi in range(N_GROUPS):
            outs[f"pk{gi}"].append(kn_p[gi])
            outs[f"pv{gi}"].append(vn_p[gi])
            outs[f"sk{gi}"].append(kn_s[gi])
            outs[f"sv{gi}"].append(vn_s[gi])
        outs["pwkv"].append(wkv_p)
        outs["swkv"].append(wkv_s)
        outs["pshift"].append(z_p.reshape(B, S, ZW)[:, -1, :rwkv_cols])
        outs["sshift"].append(z_s[:Bs, :rwkv_cols])
    st = lambda n: jnp.stack(outs[n])
    new_kv_p = [jnp.stack([st(f"pk{gi}"), st(f"pv{gi}")], axis=3) for gi in range(N_GROUPS)]
    new_kv_s = [_shift_append(caches[gi], jnp.stack([st(f"sk{gi}"), st(f"sv{gi}")], axis=3))
                for gi in range(N_GROUPS)]
    return (yp.reshape(B, S, D), ys[:Bs].reshape(Bs, 1, D), *new_kv_p, st("pwkv"), st("pshift"),
            *new_kv_s, st("swkv"), st("sshift"))
```
